```python
import jax, jax.numpy as jnp
from jax import lax
import numpy as np

D_MODEL = 2048
BATCH = 2
SEQ = 8192
DEPTH = 4
DEC_BATCH = 2
DEC_SEQ = 16384
PAST_LEN = 128

N_MIXERS = 4
N_HEADS = 16
N_KV_HEADS = 4
HEAD_DIM = D_MODEL // N_HEADS
GQA_GROUP = N_HEADS // N_KV_HEADS
Q_DIM = N_HEADS * HEAD_DIM
KV_DIM = N_KV_HEADS * HEAD_DIM
QKV_DIM = Q_DIM + 2 * KV_DIM
Q_BLOCK = 128
WINDOW = 128
ROPE_THETA = 10000.0
GRID_W = 64
N_FOURIER_GROUPS = 8
FOURIER_GROUP_CH = D_MODEL // N_FOURIER_GROUPS
CONV_W = 3
D_FF = -(-8 * D_MODEL // (3 * 256)) * 256
N_META = 16
EPS = 1e-6
NEG_INF = -1e30
N_A = len(range(0, DEPTH, N_MIXERS))
N_B = len(range(1, DEPTH, N_MIXERS))
N_C = len(range(2, DEPTH, N_MIXERS))
N_D = len(range(3, DEPTH, N_MIXERS))

kernel_name = "hybrid_bidir_encoder_interleaved"


def rms_norm(x, g):
    xf = x.astype(jnp.float32)
    y = xf * lax.rsqrt(jnp.mean(xf * xf, axis=-1, keepdims=True) + EPS)
    return (y * g.astype(jnp.float32)).astype(x.dtype)


def alibi_slopes():
    return 2.0 ** (-8.0 * jnp.arange(1, N_HEADS + 1, dtype=jnp.float32) / N_HEADS)


def qkv_heads(hn, wqkv, q_gain, k_gain):
    B, L, _ = hn.shape
    q, k, v = jnp.split(hn @ wqkv, [Q_DIM, Q_DIM + KV_DIM], axis=-1)
    q = rms_norm(q.reshape(B, L, N_HEADS, HEAD_DIM), q_gain)
    k = rms_norm(k.reshape(B, L, N_KV_HEADS, HEAD_DIM), k_gain)
    v = v.reshape(B, L, N_KV_HEADS, HEAD_DIM)
    return q, k, v


def attend(q, k, v, bias, sink):
    s = jnp.einsum('bqhgd,bkhd->bhgqk', q, k).astype(jnp.float32) * (HEAD_DIM ** -0.5)
    if bias is not None:
        s = s + bias
    if sink is None:
        p = jax.nn.softmax(s, axis=-1)
    else:
        snk = sink.astype(jnp.float32).reshape(N_KV_HEADS, GQA_GROUP)[None, :, :, None, None]
        m = jnp.maximum(jnp.max(s, axis=-1, keepdims=True), snk)
        e = jnp.exp(s - m)
        p = e / (jnp.sum(e, axis=-1, keepdims=True) + jnp.exp(snk - m))
    return jnp.einsum('bhgqk,bkhd->bqhgd', p.astype(v.dtype), v)


def axial_rope(x, row, col):
    half, quarter = HEAD_DIM // 2, HEAD_DIM // 4
    inv = ROPE_THETA ** (-jnp.arange(quarter, dtype=jnp.float32) / quarter)
    xf = x.astype(jnp.float32)

    def rot(seg, pos):
        ang = pos[:, None] * inv[None, :]
        c = jnp.cos(ang)[None, :, None, :]
        s = jnp.sin(ang)[None, :, None, :]
        s1, s2 = seg[..., :quarter], seg[..., quarter:]
        return jnp.concatenate([s1 * c - s2 * s, s1 * s + s2 * c], axis=-1)

    return jnp.concatenate([rot(xf[..., :half], row), rot(xf[..., half:], col)], axis=-1).astype(x.dtype)


def mixer_a(hn, wqkv, q_gain, k_gain, wo):
    B, L, _ = hn.shape
    S = L - N_META
    rows = S // GRID_W
    zeros = jnp.zeros((N_META,), jnp.float32)
    row = jnp.concatenate([zeros, jnp.broadcast_to(jnp.arange(rows, dtype=jnp.float32)[:, None], (rows, GRID_W)).reshape(-1)])
    col = jnp.concatenate([zeros, jnp.broadcast_to(jnp.arange(GRID_W, dtype=jnp.float32)[None, :], (rows, GRID_W)).reshape(-1)])
    q, k, v = qkv_heads(hn, wqkv, q_gain, k_gain)
    q = axial_rope(q, row, col).reshape(B, L, N_KV_HEADS, GQA_GROUP, HEAD_DIM)
    k = axial_rope(k, row, col)
    o_meta = attend(q[:, :N_META], k, v, None, None)
    n_blk = S // Q_BLOCK
    q_blk = jnp.moveaxis(q[:, N_META:].reshape(B, n_blk, Q_BLOCK, N_KV_HEADS, GQA_GROUP, HEAD_DIM), 1, 0)
    o_real = lax.map(lambda qb: attend(qb, k, v, None, None), q_blk)
    o_real = jnp.moveaxis(o_real, 0, 1).reshape(B, S, N_KV_HEADS, GQA_GROUP, HEAD_DIM)
    o = jnp.concatenate([o_meta, o_real], axis=1).reshape(B, L, Q_DIM)
    return o @ wo


def mixer_b(hn, w):
    B, L, _ = hn.shape
    hf = hn.astype(jnp.float32).reshape(B, L, N_FOURIER_GROUPS, FOURIER_GROUP_CH)
    mixed = jnp.fft.fft2(hf, axes=(1, 3), norm="ortho").real
    return mixed.reshape(B, L, D_MODEL).astype(hn.dtype) @ w


def mixer_c(hn, w_in, conv_w, w_out):
    L = hn.shape[1]
    b_gate, c_gate, u = jnp.split(hn @ w_in, 3, axis=-1)
    g = c_gate * u
    pad = CONV_W // 2
    gp = jnp.pad(g, ((0, 0), (pad, pad), (0, 0)))
    conv = sum(gp[:, i:i + L] * conv_w[i] for i in range(CONV_W))
    return (b_gate * conv) @ w_out


def mixer_d(hn, wqkv, q_gain, k_gain, sink, wo):
    B, L, _ = hn.shape
    S = L - N_META
    q, k, v = qkv_heads(hn, wqkv, q_gain, k_gain)
    q = q.reshape(B, L, N_KV_HEADS, GQA_GROUP, HEAD_DIM)
    slopes = alibi_slopes().reshape(N_KV_HEADS, GQA_GROUP)[:, :, None, None]
    k_meta, v_meta = k[:, :N_META], v[:, :N_META]
    pos_k = jnp.arange(N_META + WINDOW)
    pos_q = jnp.arange(N_META)
    ok_meta = (pos_k[None, :] < N_META) | (pos_k[None, :] - pos_q[:, None] <= WINDOW)
    bias_meta = jnp.where(ok_meta, 0.0, NEG_INF).astype(jnp.float32)
    o_meta = attend(q[:, :N_META], k[:, :N_META + WINDOW], v[:, :N_META + WINDOW], bias_meta, sink)
    band = Q_BLOCK + 2 * WINDOW
    padw = ((0, 0), (WINDOW, WINDOW), (0, 0), (0, 0))
    k_pad = jnp.pad(k[:, N_META:], padw)
    v_pad = jnp.pad(v[:, N_META:], padw)
    n_blk = S // Q_BLOCK
    qi = jnp.arange(Q_BLOCK)
    kj = jnp.arange(band)
    dist = jnp.abs(qi[:, None] - kj[None, :] + WINDOW)
    alibi = -slopes * dist.astype(jnp.float32)
    meta_bias = jnp.zeros((N_KV_HEADS, GQA_GROUP, Q_BLOCK, N_META), jnp.float32)
    q_blk = jnp.moveaxis(q[:, N_META:].reshape(B, n_blk, Q_BLOCK, N_KV_HEADS, GQA_GROUP, HEAD_DIM), 1, 0)

    def blk(args):
        qb, b = args
        start = b * Q_BLOCK
        kb = lax.dynamic_slice_in_dim(k_pad, start, band, axis=1)
        vb = lax.dynamic_slice_in_dim(v_pad, start, band, axis=1)
        j = start - WINDOW + kj
        ok = (dist <= WINDOW) & ((j >= 0) & (j < S))[None, :]
        bias = jnp.concatenate([meta_bias, jnp.where(ok, alibi, NEG_INF)], axis=-1)
        return attend(qb, jnp.concatenate([k_meta, kb], axis=1), jnp.concatenate([v_meta, vb], axis=1), bias, sink)

    o_real = lax.map(blk, (q_blk, jnp.arange(n_blk)))
    o_real = jnp.moveaxis(o_real, 0, 1).reshape(B, S, N_KV_HEADS, GQA_GROUP, HEAD_DIM)
    o = jnp.concatenate([o_meta, o_real], axis=1).reshape(B, L, Q_DIM)
    return o @ wo


def swiglu(hn, w_in, w_out):
    g, u = jnp.split(hn @ w_in, 2, axis=-1)
    return (jax.nn.silu(g) * u) @ w_out


def encode(x, meta_tokens, ln_mix, ln_ffn, a_wqkv, a_q_norm, a_k_norm, a_wo, b_w,
           c_w_in, c_conv, c_w_out, d_wqkv, d_q_norm, d_k_norm, d_sink, d_wo, ffn_w_in, ffn_w_out):
    B = x.shape[0]
    meta = jnp.broadcast_to(meta_tokens.astype(x.dtype)[None], (B, N_META, D_MODEL))
    h = jnp.concatenate([meta, x], axis=1)
    for i in range(DEPTH):
        kind, j = i % N_MIXERS, i // N_MIXERS
        hn = rms_norm(h, ln_mix[i])
        if kind == 0:
            mix = mixer_a(hn, a_wqkv[j], a_q_norm[j], a_k_norm[j], a_wo[j])
        elif kind == 1:
            mix = mixer_b(hn, b_w[j])
        elif kind == 2:
            mix = mixer_c(hn, c_w_in[j], c_conv[j], c_w_out[j])
        else:
            mix = mixer_d(hn, d_wqkv[j], d_q_norm[j], d_k_norm[j], d_sink[j], d_wo[j])
        h = h + mix
        h = h + swiglu(rms_norm(h, ln_ffn[i]), ffn_w_in[i], ffn_w_out[i])
    return h[:, N_META:]


def setup_inputs(seed: int = 0) -> dict:
    key = jax.random.key(seed)
    ks = jax.random.split(key, 24)
    f32 = jnp.float32

    def nrm(k, shape, scale):
        return jax.random.normal(k, shape, f32) * scale

    def gain(k, shape):
        return 1.0 + 0.02 * jax.random.normal(k, shape, f32)

    return {
        "x_prompt": nrm(ks[0], (BATCH, SEQ, D_MODEL), 1.0),
        "x_sample": nrm(ks[1], (DEC_BATCH, DEC_SEQ, D_MODEL), 1.0),
        "meta_tokens": nrm(ks[2], (N_META, D_MODEL), 1.0),
        "ln_mix": gain(ks[3], (DEPTH, D_MODEL)),
        "ln_ffn": gain(ks[4], (DEPTH, D_MODEL)),
        "a_wqkv": nrm(ks[5], (N_A, D_MODEL, QKV_DIM), D_MODEL ** -0.5),
        "a_q_norm": gain(ks[6], (N_A, HEAD_DIM)),
        "a_k_norm": gain(ks[7], (N_A, HEAD_DIM)),
        "a_wo": nrm(ks[8], (N_A, Q_DIM, D_MODEL), Q_DIM ** -0.5),
        "b_w": nrm(ks[9], (N_B, D_MODEL, D_MODEL), D_MODEL ** -0.5),
        "c_w_in": nrm(ks[10], (N_C, D_MODEL, 3 * D_MODEL), D_MODEL ** -0.5),
        "c_conv": nrm(ks[11], (N_C, CONV_W, D_MODEL), CONV_W ** -0.5),
        "c_w_out": nrm(ks[12], (N_C, D_MODEL, D_MODEL), D_MODEL ** -0.5),
        "d_wqkv": nrm(ks[13], (N_D, D_MODEL, QKV_DIM), D_MODEL ** -0.5),
        "d_q_norm": gain(ks[14], (N_D, HEAD_DIM)),
        "d_k_norm": gain(ks[15], (N_D, HEAD_DIM)),
        "d_sink": nrm(ks[16], (N_D, N_HEADS), 0.5),
        "d_wo": nrm(ks[17], (N_D, Q_DIM, D_MODEL), Q_DIM ** -0.5),
        "ffn_w_in": nrm(ks[18], (DEPTH, D_MODEL, 2 * D_FF), D_MODEL ** -0.5),
        "ffn_w_out": nrm(ks[19], (DEPTH, D_FF, D_MODEL), D_FF ** -0.5),
    }


def reference(x_prompt, x_sample, meta_tokens, ln_mix, ln_ffn, a_wqkv, a_q_norm, a_k_norm, a_wo, b_w,
              c_w_in, c_conv, c_w_out, d_wqkv, d_q_norm, d_k_norm, d_sink, d_wo, ffn_w_in, ffn_w_out):
    y_prompt = encode(x_prompt, meta_tokens, ln_mix, ln_ffn, a_wqkv, a_q_norm, a_k_norm, a_wo, b_w,
                      c_w_in, c_conv, c_w_out, d_wqkv, d_q_norm, d_k_norm, d_sink, d_wo, ffn_w_in, ffn_w_out)
    y_sample = encode(x_sample, meta_tokens, ln_mix, ln_ffn, a_wqkv, a_q_norm, a_k_norm, a_wo, b_w,
                      c_w_in, c_conv, c_w_out, d_wqkv, d_q_norm, d_k_norm, d_sink, d_wo, ffn_w_in, ffn_w_out)
    return (y_prompt, y_sample)
```

```python
import functools
import math

import numpy as np
import jax
import jax.numpy as jnp
from jax import lax
from jax.experimental import pallas as pl
from jax.experimental.pallas import tpu as pltpu

D = 2048
N_HEADS = 16
N_KV = 4
GROUP = N_HEADS // N_KV
DH = D // N_HEADS
QW = GROUP * DH
KV_DIM = N_KV * DH
QKV_DIM = D + 2 * KV_DIM
WINDOW = 128
ROPE_THETA = 10000.0
GRID_W = 64
FGROUP = 256
D_FF = 5632
N_META = 16
EPS = 1e-6
NEG = -1e30

FRONT = 256
META0 = FRONT - N_META
QB = 256
TM = 512
HALO = 16
VMEM_LIMIT = 52 * 1024 * 1024

bf16 = jnp.bfloat16
f32 = jnp.float32


def _cparams(sem):
    return pltpu.CompilerParams(dimension_semantics=sem, vmem_limit_bytes=VMEM_LIMIT)


def _rms(x, g):
    return x * lax.rsqrt(jnp.mean(x * x, axis=-1, keepdims=True) + EPS) * g


def _dot(a, b):
    return jnp.dot(a, b, preferred_element_type=f32)


def _dot_nt(a, b):
    return lax.dot_general(a, b, (((1,), (1,)), ((), ())), preferred_element_type=f32)


def _norm_kernel(h_ref, g_ref, o_ref):
    o_ref[...] = _rms(h_ref[...], g_ref[...]).astype(bf16)


def _norm(h, g):
    R = h.shape[0]
    return pl.pallas_call(
        _norm_kernel,
        grid=(R // TM,),
        in_specs=[pl.BlockSpec((TM, D), lambda i: (i, 0)), pl.BlockSpec((1, D), lambda i: (0, 0))],
        out_specs=pl.BlockSpec((TM, D), lambda i: (i, 0)),
        out_shape=jax.ShapeDtypeStruct((R, D), bf16),
        compiler_params=_cparams(("parallel",)),
        name="rms_norm",
    )(h, g.reshape(1, D))


def _head_norm(yh, gain):
    return yh * lax.rsqrt(jnp.mean(yh * yh, axis=-1, keepdims=True) + EPS) * gain


def _qkv_kernel(*refs, rope):
    if rope:
        h_ref, g_ref, w_ref, hg_ref, cos_ref, sa_ref, sb_ref, o_ref, xn_ref = refs
    else:
        h_ref, g_ref, w_ref, hg_ref, o_ref, xn_ref = refs
    j = pl.program_id(1)

    @pl.when(j == 0)
    def _():
        xn_ref[...] = _rms(h_ref[...], g_ref[...]).astype(bf16)

    y = _dot(xn_ref[...], w_ref[...])

    def heads(gain, scale):
        for hh in range(GROUP):
            sl = slice(hh * DH, (hh + 1) * DH)
            n = _head_norm(y[:, sl], gain)
            if rope:
                n = (n * cos_ref[...] + pltpu.roll(n, DH - 32, 1) * sa_ref[...]
                     + pltpu.roll(n, 32, 1) * sb_ref[...])
            o_ref[:, sl] = (n * scale).astype(bf16)

    @pl.when(j < D // QW)
    def _():
        heads(hg_ref[0:1, :], DH ** -0.5)

    @pl.when(j == D // QW)
    def _():
        heads(hg_ref[1:2, :], 1.0)

    @pl.when(j > D // QW)
    def _():
        o_ref[...] = y.astype(bf16)


def _qkv(h, g, w, q_gain, k_gain, rope_tabs):
    R = h.shape[0]
    rope = rope_tabs is not None
    in_specs = [
        pl.BlockSpec((TM, D), lambda i, j: (i, 0)),
        pl.BlockSpec((1, D), lambda i, j: (0, 0)),
        pl.BlockSpec((D, QW), lambda i, j: (0, j)),
        pl.BlockSpec((2, DH), lambda i, j: (0, 0)),
    ]
    args = [h, g.reshape(1, D), w, jnp.stack([q_gain, k_gain])]
    if rope:
        in_specs += [pl.BlockSpec((TM, DH), lambda i, j: (i, 0))] * 3
        args += list(rope_tabs)
    return pl.pallas_call(
        functools.partial(_qkv_kernel, rope=rope),
        grid=(R // TM, QKV_DIM // QW),
        in_specs=in_specs,
        out_specs=pl.BlockSpec((TM, QW), lambda i, j: (i, j)),
        out_shape=jax.ShapeDtypeStruct((R, QKV_DIM), bf16),
        scratch_shapes=[pltpu.VMEM((TM, D), bf16)],
        compiler_params=_cparams(("parallel", "arbitrary")),
        name="qkv_proj",
    )(*args)


def _rope_tables(S, B):
    quarter = DH // 4
    inv = ROPE_THETA ** (-jnp.arange(quarter, dtype=f32) / quarter)
    t = jnp.arange(S)
    row = jnp.concatenate([jnp.zeros((FRONT,), f32), (t // GRID_W).astype(f32)])
    col = jnp.concatenate([jnp.zeros((FRONT,), f32), (t % GRID_W).astype(f32)])
    ang_r = row[:, None] * inv[None, :]
    ang_c = col[:, None] * inv[None, :]
    ang = jnp.concatenate([ang_r, ang_r, ang_c, ang_c], axis=1)
    cos, sin = jnp.cos(ang), jnp.sin(ang)
    first = (np.arange(DH) % 64) < 32
    sa = jnp.where(first[None, :], -sin, 0.0)
    sb = jnp.where(first[None, :], 0.0, sin)
    tile = lambda a: jnp.tile(a, (B, 1))
    return tile(cos), tile(sa), tile(sb)


def _flash_kernel(q_ref, k_ref, v_ref, o_ref, *, n_chunks, tk):
    qi = pl.program_id(2)
    col0 = lax.broadcasted_iota(jnp.int32, (QB, FRONT), 1)
    rowi = lax.broadcasted_iota(jnp.int32, (QB, DH), 0)
    keep_rows = jnp.logical_or(qi > 0, rowi >= META0)
    for hh in range(GROUP):
        sl = slice(hh * DH, (hh + 1) * DH)
        q = q_ref[:, sl]
        s = _dot_nt(q, k_ref[0:FRONT, :])
        s = jnp.where(col0 >= META0, s, NEG)
        m = jnp.max(s, axis=-1, keepdims=True)
        p = jnp.exp(s - m)
        l = jnp.sum(p, axis=-1, keepdims=True)
        acc = _dot(p.astype(bf16), v_ref[0:FRONT, :])

        def body(c, carry):
            m, l, acc = carry
            start = pl.multiple_of(FRONT + c * tk, FRONT)
            s = _dot_nt(q, k_ref[pl.ds(start, tk), :])
            m_new = jnp.maximum(m, jnp.max(s, axis=-1, keepdims=True))
            alpha = jnp.exp(m - m_new)
            p = jnp.exp(s - m_new)
            l = alpha * l + jnp.sum(p, axis=-1, keepdims=True)
            acc = alpha * acc + _dot(p.astype(bf16), v_ref[pl.ds(start, tk), :])
            return m_new, l, acc

        m, l, acc = lax.fori_loop(0, n_chunks, body, (m, l, acc))
        o_ref[:, sl] = jnp.where(keep_rows, acc / l, 0.0).astype(bf16)


def _flash(qkv, B, Lp):
    R = qkv.shape[0]
    S = Lp - FRONT
    tk = 512 if S % 512 == 0 else 256
    nq = Lp // QB
    kcol, vcol = D // DH, (D + KV_DIM) // DH
    return pl.pallas_call(
        functools.partial(_flash_kernel, n_chunks=S // tk, tk=tk),
        grid=(B, N_KV, nq),
        in_specs=[
            pl.BlockSpec((QB, QW), lambda b, g, i: (b * nq + i, g)),
            pl.BlockSpec((Lp, DH), lambda b, g, i: (b, kcol + g)),
            pl.BlockSpec((Lp, DH), lambda b, g, i: (b, vcol + g)),
        ],
        out_specs=pl.BlockSpec((QB, QW), lambda b, g, i: (b * nq + i, g)),
        out_shape=jax.ShapeDtypeStruct((R, D), bf16),
        compiler_params=_cparams(("parallel", "parallel", "arbitrary")),
        name="dense_attention",
    )(qkv, qkv, qkv)


def _win_kernel(hs_ref, q_ref, k_ref, v_ref, o_ref, *, Lp):
    g = pl.program_id(1)
    qi = pl.program_id(2)
    band = 2 * QB
    start = pl.multiple_of(jnp.clip(QB * qi - WINDOW, 0, Lp - band), WINDOW)
    kb = k_ref[pl.ds(start, band), :]
    vb = v_ref[pl.ds(start, band), :]
    km = k_ref[FRONT - DH:FRONT, :]
    vm = v_ref[FRONT - DH:FRONT, :]
    meta_ok = lax.broadcasted_iota(jnp.int32, (QB, DH), 1) >= DH - N_META
    qpos = QB * qi + lax.broadcasted_iota(jnp.int32, (QB, band), 0)
    kpos = start + lax.broadcasted_iota(jnp.int32, (QB, band), 1)
    dist = jnp.abs(qpos - kpos)
    ok = jnp.logical_and(dist <= WINDOW, kpos >= FRONT)
    distf = dist.astype(f32)
    rowi = lax.broadcasted_iota(jnp.int32, (QB, DH), 0)
    keep_rows = jnp.logical_or(qi > 0, rowi >= META0)
    for hh in range(GROUP):
        sl = slice(hh * DH, (hh + 1) * DH)
        q = q_ref[:, sl]
        slope = jnp.where(qi == 0, 0.0, hs_ref[0, g * GROUP + hh])
        sink = hs_ref[1, g * GROUP + hh]
        sb = jnp.where(ok, _dot_nt(q, kb) - slope * distf, NEG)
        sm = jnp.where(meta_ok, _dot_nt(q, km), NEG)
        m = jnp.maximum(jnp.max(sb, axis=-1, keepdims=True), jnp.max(sm, axis=-1, keepdims=True))
        m = jnp.maximum(m, sink)
        eb = jnp.exp(sb - m)
        em = jnp.exp(sm - m)
        den = (jnp.sum(eb, axis=-1, keepdims=True) + jnp.sum(em, axis=-1, keepdims=True)
               + jnp.exp(sink - m))
        acc = _dot(eb.astype(bf16), vb) + _dot(em.astype(bf16), vm)
        o_ref[:, sl] = jnp.where(keep_rows, acc / den, 0.0).astype(bf16)


def _win_attention(qkv, sink, B, Lp):
    R = qkv.shape[0]
    nq = Lp // QB
    kcol, vcol = D // DH, (D + KV_DIM) // DH
    slopes = 2.0 ** (-8.0 * jnp.arange(1, N_HEADS + 1, dtype=f32) / N_HEADS)
    hs = jnp.stack([slopes, sink.astype(f32)])
    return pl.pallas_call(
        functools.partial(_win_kernel, Lp=Lp),
        grid=(B, N_KV, nq),
        in_specs=[
            pl.BlockSpec(memory_space=pltpu.SMEM),
            pl.BlockSpec((QB, QW), lambda b, g, i: (b * nq + i, g)),
            pl.BlockSpec((Lp, DH), lambda b, g, i: (b, kcol + g)),
            pl.BlockSpec((Lp, DH), lambda b, g, i: (b, vcol + g)),
        ],
        out_specs=pl.BlockSpec((QB, QW), lambda b, g, i: (b * nq + i, g)),
        out_shape=jax.ShapeDtypeStruct((R, D), bf16),
        compiler_params=_cparams(("parallel", "parallel", "arbitrary")),
        name="window_attention",
    )(hs, qkv, qkv, qkv)


def _proj_kernel(h_ref, a_ref, w_ref, o_ref):
    o_ref[...] = h_ref[...] + _dot(a_ref[...], w_ref[...])


def _proj_residual(h, a, w):
    R = h.shape[0]
    return pl.pallas_call(
        _proj_kernel,
        grid=(R // TM,),
        in_specs=[
            pl.BlockSpec((TM, D), lambda i: (i, 0)),
            pl.BlockSpec((TM, D), lambda i: (i, 0)),
            pl.BlockSpec((D, D), lambda i: (0, 0)),
        ],
        out_specs=pl.BlockSpec((TM, D), lambda i: (i, 0)),
        out_shape=jax.ShapeDtypeStruct((R, D), f32),
        compiler_params=_cparams(("parallel",)),
        name="proj_residual",
    )(h, a, w)


TF = 512


def _ffn_kernel(h_ref, g_ref, wg_ref, wu_ref, wo_ref, o_ref, xn_ref):
    @pl.when(pl.program_id(1) == 0)
    def _():
        x = h_ref[...]
        xn_ref[...] = _rms(x, g_ref[...]).astype(bf16)
        o_ref[...] = x

    xn = xn_ref[...]
    gate = _dot(xn, wg_ref[...])
    up = _dot(xn, wu_ref[...])
    a = gate * jax.nn.sigmoid(gate) * up
    o_ref[...] += _dot(a.astype(bf16), wo_ref[...])


def _ffn(h, g, w_in, w_out):
    R = h.shape[0]
    nf = D_FF // TF
    return pl.pallas_call(
        _ffn_kernel,
        grid=(R // TM, nf),
        in_specs=[
            pl.BlockSpec((TM, D), lambda i, f: (i, 0)),
            pl.BlockSpec((1, D), lambda i, f: (0, 0)),
            pl.BlockSpec((D, TF), lambda i, f: (0, f)),
            pl.BlockSpec((D, TF), lambda i, f: (0, nf + f)),
            pl.BlockSpec((TF, D), lambda i, f: (f, 0)),
        ],
        out_specs=pl.BlockSpec((TM, D), lambda i, f: (i, 0)),
        out_shape=jax.ShapeDtypeStruct((R, D), f32),
        scratch_shapes=[pltpu.VMEM((TM, D), bf16)],
        compiler_params=_cparams(("parallel", "arbitrary")),
        name="swiglu_ffn",
    )(h, g.reshape(1, D), w_in, w_in, w_out)


TC = 512


def _conv_kernel(h_ref, hp_ref, hn_ref, g_ref, wb_ref, wc_ref, wu_ref, cw_ref, wo_ref, o_ref,
                 xn_ref, gu_ref):
    i = pl.program_id(0)

    @pl.when(pl.program_id(1) == 0)
    def _():
        x = h_ref[...]
        gain = g_ref[...]
        xn_ref[HALO:HALO + TM, :] = _rms(x, gain).astype(bf16)
        xn_ref[0:HALO, :] = _rms(hp_ref[...], gain).astype(bf16)
        nxt = _rms(hn_ref[...], gain)
        nxt = jnp.where(i == pl.num_programs(0) - 1, 0.0, nxt)
        xn_ref[HALO + TM:, :] = nxt.astype(bf16)
        o_ref[...] = x

    xn = xn_ref[...]
    gu_ref[...] = _dot(xn, wc_ref[...]) * _dot(xn, wu_ref[...])
    cw = cw_ref[...]
    conv = (gu_ref[HALO - 1:HALO - 1 + TM, :] * cw[0:1, :]
            + gu_ref[HALO:HALO + TM, :] * cw[1:2, :]
            + gu_ref[HALO + 1:HALO + 1 + TM, :] * cw[2:3, :])
    bgate = _dot(xn_ref[HALO:HALO + TM, :], wb_ref[...])
    o_ref[...] += _dot((bgate * conv).astype(bf16), wo_ref[...])


def _conv_mixer(h, g, w_in, conv_w, w_out):
    R = h.shape[0]
    nc = D // TC
    hb = TM // HALO
    last = R // HALO - 1
    return pl.pallas_call(
        _conv_kernel,
        grid=(R // TM, nc),
        in_specs=[
            pl.BlockSpec((TM, D), lambda i, c: (i, 0)),
            pl.BlockSpec((HALO, D), lambda i, c: (jnp.maximum(i * hb - 1, 0), 0)),
            pl.BlockSpec((HALO, D), lambda i, c: (jnp.minimum((i + 1) * hb, last), 0)),
            pl.BlockSpec((1, D), lambda i, c: (0, 0)),
            pl.BlockSpec((D, TC), lambda i, c: (0, c)),
            pl.BlockSpec((D, TC), lambda i, c: (0, nc + c)),
            pl.BlockSpec((D, TC), lambda i, c: (0, 2 * nc + c)),
            pl.BlockSpec((3, TC), lambda i, c: (0, c)),
            pl.BlockSpec((TC, D), lambda i, c: (c, 0)),
        ],
        out_specs=pl.BlockSpec((TM, D), lambda i, c: (i, 0)),
        out_shape=jax.ShapeDtypeStruct((R, D), f32),
        scratch_shapes=[pltpu.VMEM((TM + 2 * HALO, D), bf16), pltpu.VMEM((TM + 2 * HALO, TC), f32)],
        compiler_params=_cparams(("parallel", "arbitrary")),
        name="gated_conv",
    )(h, h, h, g.reshape(1, D), w_in, w_in, w_in, conv_w, w_out)


def _dft_factors(L):
    best = None
    for l1 in range(16, L, 16):
        if L % l1:
            continue
        l2p = -(-(L // l1) // 16) * 16
        cost = (2 * 256 * -(-l1 // 256) + 4 * 256 * -(-l2p // 256), 2 * l1 + 4 * l2p)
        if best is None or cost < best[0]:
            best = (cost, l1, L // l1, l2p)
    return best[1:]


def _dft_tables(L, L1, L2, L2p):
    def cs(num, den):
        ang = (2.0 * math.pi / den) * (num % den).astype(f32)
        return jnp.cos(ang), jnp.sin(ang)

    i1 = jnp.arange(L1)
    c1, s1 = cs(i1[:, None] * i1[None, :], L1)
    w1 = jnp.concatenate([c1, -s1], axis=0).astype(bf16)
    n2 = jnp.arange(L2p)
    tc, ts = cs(n2[:, None] * i1[None, :], L)
    live = (n2 < L2)[:, None] * (L ** -0.5)
    tw = jnp.stack([tc * live, ts * live], axis=1)[..., None]
    c3, s3 = cs(n2[:, None] * n2[None, :], L2)
    inside = jnp.logical_and(n2[:, None] < L2, n2[None, :] < L2)
    w3 = jnp.stack([jnp.where(inside, c3, 0.0), jnp.where(inside, s3, 0.0)]).astype(bf16)
    ic = jnp.arange(FGROUP)
    cc, sc = cs(ic[:, None] * ic[None, :], FGROUP)
    wc = (jnp.stack([cc, sc]) * (FGROUP ** -0.5)).astype(bf16)
    return w1, tw, w3, wc


def _dft1_kernel(x_ref, w_ref, tw_ref, tr_ref, ti_ref, *, L1):
    t = _dot(w_ref[...], x_ref[...])
    tr, ti = t[:L1], t[L1:]
    tc, ts = tw_ref[0], tw_ref[1]
    tr_ref[...] = (tr * tc + ti * ts).astype(bf16)
    ti_ref[...] = (ti * tc - tr * ts).astype(bf16)


def _dft3_kernel(tr_ref, ti_ref, w3_ref, wc_ref, o_ref):
    c3, s3 = w3_ref[0], w3_ref[1]
    tr, ti = tr_ref[...], ti_ref[...]
    yr = (_dot(c3, tr) + _dot(s3, ti)).astype(bf16)
    yi = (_dot(c3, ti) - _dot(s3, tr)).astype(bf16)
    cc, sc = wc_ref[0], wc_ref[1]
    for gidx in range(D // FGROUP):
        sl = slice(gidx * FGROUP, (gidx + 1) * FGROUP)
        o_ref[:, sl] = (_dot(yr[:, sl], cc) + _dot(yi[:, sl], sc)).astype(bf16)


def _fourier_mix(hn, B, Lp):
    L = Lp - META0
    L1, L2, L2p = _dft_factors(L)
    w1, tw, w3, wc = _dft_tables(L, L1, L2, L2p)
    x = hn.reshape(B, Lp, D)[:, META0:].reshape(B, L1, L2 * D)
    t_shape = jax.ShapeDtypeStruct((B, L1, L2p * D), bf16)
    tr, ti = pl.pallas_call(
        functools.partial(_dft1_kernel, L1=L1),
        grid=(B, L2p),
        in_specs=[
            pl.BlockSpec((None, L1, D), lambda b, n: (b, 0, jnp.minimum(n, L2 - 1))),
            pl.BlockSpec((2 * L1, L1), lambda b, n: (0, 0)),
            pl.BlockSpec((None, 2, L1, 1), lambda b, n: (n, 0, 0, 0)),
        ],
        out_specs=[pl.BlockSpec((None, L1, D), lambda b, n: (b, 0, n))] * 2,
        out_shape=[t_shape, t_shape],
        compiler_params=_cparams(("parallel", "arbitrary")),
        name="seq_dft_stage1",
    )(x, w1, tw)
    tr = tr.reshape(B, L1, L2p, D)
    ti = ti.reshape(B, L1, L2p, D)
    mixed = pl.pallas_call(
        _dft3_kernel,
        grid=(B, L1),
        in_specs=[
            pl.BlockSpec((None, None, L2p, D), lambda b, k: (b, k, 0, 0)),
            pl.BlockSpec((None, None, L2p, D), lambda b, k: (b, k, 0, 0)),
            pl.BlockSpec((2, L2p, L2p), lambda b, k: (0, 0, 0)),
            pl.BlockSpec((2, FGROUP, FGROUP), lambda b, k: (0, 0, 0)),
        ],
        out_specs=pl.BlockSpec((None, L2p, D), lambda b, k: (b, 0, k)),
        out_shape=jax.ShapeDtypeStruct((B, L2p, L1 * D), bf16),
        compiler_params=_cparams(("parallel", "arbitrary")),
        name="seq_dft_stage2_channel_dft",
    )(tr, ti, w3, wc)
    mixed = mixed.reshape(B, L2p * L1, D)[:, :L]
    return jnp.pad(mixed, ((0, 0), (META0, 0), (0, 0))).reshape(B * Lp, D)


def _encode(x, meta, p):
    B, S, _ = x.shape
    Lp = S + FRONT
    front = jnp.concatenate([jnp.zeros((META0, D), f32), meta.astype(f32)], axis=0)
    h = jnp.concatenate([jnp.broadcast_to(front[None], (B, FRONT, D)), x], axis=1).reshape(B * Lp, D)

    qkv = _qkv(h, p["ln_mix"][0], p["a_wqkv"], p["a_q_norm"], p["a_k_norm"], _rope_tables(S, B))
    h = _proj_residual(h, _flash(qkv, B, Lp), p["a_wo"])
    h = _ffn(h, p["ln_ffn"][0], p["ffn_w_in"][0], p["ffn_w_out"][0])
    h = _proj_residual(h, _fourier_mix(_norm(h, p["ln_mix"][1]), B, Lp), p["b_w"])
    h = _ffn(h, p["ln_ffn"][1], p["ffn_w_in"][1], p["ffn_w_out"][1])
    h = _conv_mixer(h, p["ln_mix"][2], p["c_w_in"], p["c_conv"], p["c_w_out"])
    h = _ffn(h, p["ln_ffn"][2], p["ffn_w_in"][2], p["ffn_w_out"][2])
    qkv = _qkv(h, p["ln_mix"][3], p["d_wqkv"], p["d_q_norm"], p["d_k_norm"], None)
    h = _proj_residual(h, _win_attention(qkv, p["d_sink"], B, Lp), p["d_wo"])
    h = _ffn(h, p["ln_ffn"][3], p["ffn_w_in"][3], p["ffn_w_out"][3])
    return h.reshape(B, Lp, D)[:, FRONT:]


def kernel(x_prompt, x_sample, meta_tokens, ln_mix, ln_ffn, a_wqkv, a_q_norm, a_k_norm, a_wo, b_w,
           c_w_in, c_conv, c_w_out, d_wqkv, d_q_norm, d_k_norm, d_sink, d_wo, ffn_w_in, ffn_w_out):
    p = {
        "ln_mix": ln_mix, "ln_ffn": ln_ffn,
        "a_wqkv": a_wqkv[0].astype(bf16), "a_q_norm": a_q_norm[0], "a_k_norm": a_k_norm[0],
        "a_wo": a_wo[0].astype(bf16), "b_w": b_w[0].astype(bf16),
        "c_w_in": c_w_in[0].astype(bf16), "c_conv": c_conv[0], "c_w_out": c_w_out[0].astype(bf16),
        "d_wqkv": d_wqkv[0].astype(bf16), "d_q_norm": d_q_norm[0], "d_k_norm": d_k_norm[0],
        "d_sink": d_sink[0], "d_wo": d_wo[0].astype(bf16),
        "ffn_w_in": [ffn_w_in[i].astype(bf16) for i in range(ffn_w_in.shape[0])],
        "ffn_w_out": [ffn_w_out[i].astype(bf16) for i in range(ffn_w_out.shape[0])],
    }
    return _encode(x_prompt, meta_tokens, p), _encode(x_sample, meta_tokens, p)
```

```python
import functools
import math

import numpy as np
import jax
import jax.numpy as jnp
from jax import lax
from jax.experimental import pallas as pl
from jax.experimental.pallas import tpu as pltpu

D = 2048
N_HEADS = 16
N_KV = 4
GROUP = N_HEADS // N_KV
DH = D // N_HEADS
QW = GROUP * DH
KV_DIM = N_KV * DH
QKV_DIM = D + 2 * KV_DIM
WINDOW = 128
ROPE_THETA = 10000.0
GRID_W = 64
FGROUP = 256
D_FF = 5632
N_META = 16
EPS = 1e-6
NEG = -1e30
LOG2E = 1.4426950408889634
Q_SCALE = DH ** -0.5 * LOG2E

FRONT = 256
META0 = FRONT - N_META
QB = 256
TM = 512
HALO = 16
SKEW = 1
VMEM_LIMIT = 52 * 1024 * 1024

bf16 = jnp.bfloat16
f32 = jnp.float32


def _cparams(sem):
    return pltpu.CompilerParams(dimension_semantics=sem, vmem_limit_bytes=VMEM_LIMIT)


def _rms(x, g):
    return x * lax.rsqrt(jnp.mean(x * x, axis=-1, keepdims=True) + EPS) * g


def _dot(a, b):
    return jnp.dot(a, b, preferred_element_type=f32)


def _dot_nt(a, b):
    return lax.dot_general(a, b, (((1,), (1,)), ((), ())), preferred_element_type=f32)


def _norm_kernel(h_ref, g_ref, o_ref):
    o_ref[...] = _rms(h_ref[...], g_ref[...]).astype(bf16)


def _norm(h, g):
    R = h.shape[0]
    return pl.pallas_call(
        _norm_kernel,
        grid=(R // TM,),
        in_specs=[pl.BlockSpec((TM, D), lambda i: (i, 0)), pl.BlockSpec((1, D), lambda i: (0, 0))],
        out_specs=pl.BlockSpec((TM, D), lambda i: (i, 0)),
        out_shape=jax.ShapeDtypeStruct((R, D), bf16),
        compiler_params=_cparams(("parallel",)),
        name="rms_norm",
    )(h, g.reshape(1, D))


def _head_norm(yh, gain):
    return yh * lax.rsqrt(jnp.mean(yh * yh, axis=-1, keepdims=True) + EPS) * gain


def _qkv_kernel(*refs, rope):
    if rope:
        h_ref, g_ref, w_ref, hg_ref, cos_ref, sin_ref, o_ref, xn_ref = refs
    else:
        h_ref, g_ref, w_ref, hg_ref, o_ref, xn_ref = refs
    xn_ref[...] = _rms(h_ref[...], g_ref[...]).astype(bf16)
    n_q = D // QW

    def project(j):
        return _dot(xn_ref[...], w_ref[:, j * QW:(j + 1) * QW])

    def finish(j, y):
        if j > n_q:
            o_ref[:, j * QW:(j + 1) * QW] = y.astype(bf16)
            return
        gain = hg_ref[0:1, :] if j < n_q else hg_ref[1:2, :]
        scale = Q_SCALE if j < n_q else 1.0
        for hh in range(GROUP):
            n = _head_norm(y[:, hh * DH:(hh + 1) * DH], gain)
            if rope:
                n = n * cos_ref[...] + pltpu.roll(n, DH // 2, 1) * sin_ref[...]
            o_ref[:, j * QW + hh * DH:j * QW + (hh + 1) * DH] = (n * scale).astype(bf16)

    n_tiles = QKV_DIM // QW
    y = project(0)
    for j in range(n_tiles):
        y_next = project(j + 1) if j + 1 < n_tiles else None
        finish(j, y)
        y = y_next


def _qkv(h, g, w, q_gain, k_gain, rope_tabs):
    R = h.shape[0]
    rope = rope_tabs is not None
    in_specs = [
        pl.BlockSpec((TM, D), lambda i: (i, 0)),
        pl.BlockSpec((1, D), lambda i: (0, 0)),
        pl.BlockSpec((D, QKV_DIM), lambda i: (0, 0)),
        pl.BlockSpec((2, DH), lambda i: (0, 0)),
    ]
    args = [h, g.reshape(1, D), w, jnp.stack([q_gain, k_gain])]
    if rope:
        in_specs += [pl.BlockSpec((TM, DH), lambda i: (i, 0))] * 2
        args += list(rope_tabs)
    return pl.pallas_call(
        functools.partial(_qkv_kernel, rope=rope),
        grid=(R // TM,),
        in_specs=in_specs,
        out_specs=pl.BlockSpec((TM, QKV_DIM), lambda i: (i, 0)),
        out_shape=jax.ShapeDtypeStruct((R, QKV_DIM), bf16),
        scratch_shapes=[pltpu.VMEM((TM, D), bf16)],
        compiler_params=_cparams(("parallel",)),
        name="qkv_proj",
    )(*args)


ROPE_PERM = np.concatenate([np.arange(0, 32), np.arange(64, 96), np.arange(32, 64), np.arange(96, 128)])


def _rope_permute(wqkv, q_gain, k_gain):
    n_qk = N_HEADS + N_KV
    cols = (np.arange(n_qk)[:, None] * DH + ROPE_PERM[None, :]).reshape(-1)
    cols = np.concatenate([cols, np.arange(n_qk * DH, QKV_DIM)])
    return wqkv[:, cols], q_gain[ROPE_PERM], k_gain[ROPE_PERM]


def _rope_tables(S, B):
    quarter = DH // 4
    inv = ROPE_THETA ** (-jnp.arange(quarter, dtype=f32) / quarter)
    t = jnp.arange(S)
    row = jnp.concatenate([jnp.zeros((FRONT,), f32), (t // GRID_W).astype(f32)])
    col = jnp.concatenate([jnp.zeros((FRONT,), f32), (t % GRID_W).astype(f32)])
    ang_r = row[:, None] * inv[None, :]
    ang_c = col[:, None] * inv[None, :]
    ang = jnp.concatenate([ang_r, ang_c, ang_r, ang_c], axis=1)
    cos, sin = jnp.cos(ang), jnp.sin(ang)
    sin = jnp.where((np.arange(DH) < DH // 2)[None, :], -sin, sin)
    return jnp.tile(cos, (B, 1)), jnp.tile(sin, (B, 1))


def _flash_kernel(q_ref, k_ref, v_ref, o_ref, m_ref, l_ref, acc_ref, *, n_chunks, tk):
    qi = pl.program_id(2)
    col0 = lax.broadcasted_iota(jnp.int32, (QB, FRONT), 1)
    k0 = k_ref[0:FRONT, :]
    v0 = v_ref[0:FRONT, :]
    for hh in range(GROUP):
        s = jnp.where(col0 >= META0, _dot_nt(q_ref[:, hh * DH:(hh + 1) * DH], k0), NEG)
        m = jnp.max(s, axis=-1, keepdims=True)
        p = jnp.exp2(s - m)
        m_ref[hh] = jnp.broadcast_to(m, (QB, DH))
        l_ref[hh] = jnp.broadcast_to(jnp.sum(p, axis=-1, keepdims=True), (QB, DH))
        acc_ref[hh] = _dot(p.astype(bf16), v0)

    def scores(hh, kc):
        return _dot_nt(q_ref[:, hh * DH:(hh + 1) * DH], kc)

    def update(hh, s, vc):
        m_prev = m_ref[hh]
        m_new = jnp.maximum(m_prev, jnp.max(s, axis=-1, keepdims=True))
        p = jnp.exp2(s - jnp.concatenate([m_new] * (tk // DH), axis=1))
        alpha = jnp.exp2(m_prev - m_new)
        l_ref[hh] = alpha * l_ref[hh] + jnp.sum(p, axis=-1, keepdims=True)
        acc_ref[hh] = alpha * acc_ref[hh] + _dot(p.astype(bf16), vc)
        m_ref[hh] = m_new

    def chunk(kc, vc, pending, k_next):
        pending = list(pending)
        for hh in range(GROUP):
            ahead = hh + SKEW
            if ahead < GROUP:
                pending.append(scores(ahead, kc))
            elif k_next is not None:
                pending.append(scores(ahead - GROUP, k_next))
            update(hh, pending.pop(0), vc)
        return tuple(pending)

    def body(c, pending):
        start = pl.multiple_of(FRONT + c * tk, FRONT)
        return chunk(k_ref[pl.ds(start, tk), :], v_ref[pl.ds(start, tk), :], pending,
                     k_ref[pl.ds(start + tk, tk), :])

    k_first = k_ref[pl.ds(FRONT, tk), :]
    pending = lax.fori_loop(0, n_chunks - 1, body, tuple(scores(hh, k_first) for hh in range(SKEW)))
    last = FRONT + (n_chunks - 1) * tk
    chunk(k_ref[pl.ds(last, tk), :], v_ref[pl.ds(last, tk), :], pending, None)
    rowi = lax.broadcasted_iota(jnp.int32, (QB, DH), 0)
    keep_rows = jnp.logical_or(qi > 0, rowi >= META0)
    for hh in range(GROUP):
        o_ref[:, hh * DH:(hh + 1) * DH] = jnp.where(keep_rows, acc_ref[hh] / l_ref[hh], 0.0).astype(bf16)


def _flash(qkv, B, Lp):
    R = qkv.shape[0]
    S = Lp - FRONT
    tk = 512 if S % 512 == 0 else 256
    nq = Lp // QB
    kcol, vcol = D // DH, (D + KV_DIM) // DH
    return pl.pallas_call(
        functools.partial(_flash_kernel, n_chunks=S // tk, tk=tk),
        grid=(B, N_KV, nq),
        in_specs=[
            pl.BlockSpec((QB, QW), lambda b, g, i: (b * nq + i, g)),
            pl.BlockSpec((Lp, DH), lambda b, g, i: (b, kcol + g)),
            pl.BlockSpec((Lp, DH), lambda b, g, i: (b, vcol + g)),
        ],
        out_specs=pl.BlockSpec((QB, QW), lambda b, g, i: (b * nq + i, g)),
        out_shape=jax.ShapeDtypeStruct((R, D), bf16),
        scratch_shapes=[pltpu.VMEM((GROUP, QB, DH), f32)] * 3,
        compiler_params=_cparams(("parallel", "parallel", "arbitrary")),
        name="dense_attention",
    )(qkv, qkv, qkv)


def _win_kernel(hs_ref, q_ref, k_ref, v_ref, o_ref, *, Lp):
    g = pl.program_id(1)
    qi = pl.program_id(2)
    band = 2 * QB
    start = pl.multiple_of(jnp.clip(QB * qi - WINDOW, 0, Lp - band), WINDOW)
    kb = k_ref[pl.ds(start, band), :]
    vb = v_ref[pl.ds(start, band), :]
    km = k_ref[FRONT - DH:FRONT, :]
    vm = v_ref[FRONT - DH:FRONT, :]
    meta_ok = lax.broadcasted_iota(jnp.int32, (QB, DH), 1) >= DH - N_META
    qpos = QB * qi + lax.broadcasted_iota(jnp.int32, (QB, band), 0)
    kpos = start + lax.broadcasted_iota(jnp.int32, (QB, band), 1)
    dist = jnp.abs(qpos - kpos)
    ok = jnp.logical_and(dist <= WINDOW, kpos >= FRONT)
    distf = dist.astype(f32)
    rowi = lax.broadcasted_iota(jnp.int32, (QB, DH), 0)
    keep_rows = jnp.logical_or(qi > 0, rowi >= META0)
    def scores(hh):
        q = q_ref[:, hh * DH:(hh + 1) * DH]
        return _dot_nt(q, kb), _dot_nt(q, km)

    nxt = scores(0)
    for hh in range(GROUP):
        raw_b, raw_m = nxt
        if hh + 1 < GROUP:
            nxt = scores(hh + 1)
        slope = jnp.where(qi == 0, 0.0, hs_ref[0, g * GROUP + hh])
        sink = hs_ref[1, g * GROUP + hh]
        sb = jnp.where(ok, raw_b - slope * distf, NEG)
        sm = jnp.where(meta_ok, raw_m, NEG)
        m = jnp.maximum(jnp.max(sb, axis=-1, keepdims=True), jnp.max(sm, axis=-1, keepdims=True))
        m = jnp.maximum(m, sink)
        eb = jnp.exp2(sb - m)
        em = jnp.exp2(sm - m)
        den = (jnp.sum(eb, axis=-1, keepdims=True) + jnp.sum(em, axis=-1, keepdims=True)
               + jnp.exp2(sink - m))
        acc = _dot(eb.astype(bf16), vb) + _dot(em.astype(bf16), vm)
        o_ref[:, hh * DH:(hh + 1) * DH] = jnp.where(keep_rows, acc / den, 0.0).astype(bf16)


def _win_attention(qkv, sink, B, Lp):
    R = qkv.shape[0]
    nq = Lp // QB
    kcol, vcol = D // DH, (D + KV_DIM) // DH
    slopes = 2.0 ** (-8.0 * jnp.arange(1, N_HEADS + 1, dtype=f32) / N_HEADS)
    hs = jnp.stack([slopes, sink.astype(f32)]) * LOG2E
    return pl.pallas_call(
        functools.partial(_win_kernel, Lp=Lp),
        grid=(B, N_KV, nq),
        in_specs=[
            pl.BlockSpec(memory_space=pltpu.SMEM),
            pl.BlockSpec((QB, QW), lambda b, g, i: (b * nq + i, g)),
            pl.BlockSpec((Lp, DH), lambda b, g, i: (b, kcol + g)),
            pl.BlockSpec((Lp, DH), lambda b, g, i: (b, vcol + g)),
        ],
        out_specs=pl.BlockSpec((QB, QW), lambda b, g, i: (b * nq + i, g)),
        out_shape=jax.ShapeDtypeStruct((R, D), bf16),
        compiler_params=_cparams(("parallel", "parallel", "arbitrary")),
        name="window_attention",
    )(hs, qkv, qkv, qkv)


def _proj_kernel(h_ref, a_ref, w_ref, o_ref):
    o_ref[...] = h_ref[...] + _dot(a_ref[...], w_ref[...])


def _proj_residual(h, a, w):
    R = h.shape[0]
    return pl.pallas_call(
        _proj_kernel,
        grid=(R // TM,),
        in_specs=[
            pl.BlockSpec((TM, D), lambda i: (i, 0)),
            pl.BlockSpec((TM, D), lambda i: (i, 0)),
            pl.BlockSpec((D, D), lambda i: (0, 0)),
        ],
        out_specs=pl.BlockSpec((TM, D), lambda i: (i, 0)),
        out_shape=jax.ShapeDtypeStruct((R, D), f32),
        compiler_params=_cparams(("parallel",)),
        name="proj_residual",
    )(h, a, w)


TF = 512


def _ffn_kernel(h_ref, g_ref, wg_ref, wu_ref, wo_ref, o_ref, xn_ref):
    @pl.when(pl.program_id(1) == 0)
    def _():
        x = h_ref[...]
        xn_ref[...] = _rms(x, g_ref[...]).astype(bf16)
        o_ref[...] = x

    xn = xn_ref[...]
    gate = _dot(xn, wg_ref[...])
    up = _dot(xn, wu_ref[...])
    a = gate * jax.nn.sigmoid(gate) * up
    o_ref[...] += _dot(a.astype(bf16), wo_ref[...])


def _ffn(h, g, w_in, w_out):
    R = h.shape[0]
    nf = D_FF // TF
    return pl.pallas_call(
        _ffn_kernel,
        grid=(R // TM, nf),
        in_specs=[
            pl.BlockSpec((TM, D), lambda i, f: (i, 0)),
            pl.BlockSpec((1, D), lambda i, f: (0, 0)),
            pl.BlockSpec((D, TF), lambda i, f: (0, f)),
            pl.BlockSpec((D, TF), lambda i, f: (0, nf + f)),
            pl.BlockSpec((TF, D), lambda i, f: (f, 0)),
        ],
        out_specs=pl.BlockSpec((TM, D), lambda i, f: (i, 0)),
        out_shape=jax.ShapeDtypeStruct((R, D), f32),
        scratch_shapes=[pltpu.VMEM((TM, D), bf16)],
        compiler_params=_cparams(("parallel", "arbitrary")),
        name="swiglu_ffn",
    )(h, g.reshape(1, D), w_in, w_in, w_out)


TC = 512


def _conv_kernel(h_ref, hp_ref, hn_ref, g_ref, wb_ref, wc_ref, wu_ref, cw_ref, wo_ref, o_ref,
                 xn_ref, gu_ref):
    i = pl.program_id(0)

    @pl.when(pl.program_id(1) == 0)
    def _():
        x = h_ref[...]
        gain = g_ref[...]
        xn_ref[HALO:HALO + TM, :] = _rms(x, gain).astype(bf16)
        xn_ref[0:HALO, :] = _rms(hp_ref[...], gain).astype(bf16)
        nxt = _rms(hn_ref[...], gain)
        nxt = jnp.where(i == pl.num_programs(0) - 1, 0.0, nxt)
        xn_ref[HALO + TM:, :] = nxt.astype(bf16)
        o_ref[...] = x

    xn = xn_ref[...]
    gu_ref[...] = _dot(xn, wc_ref[...]) * _dot(xn, wu_ref[...])
    cw = cw_ref[...]
    conv = (gu_ref[HALO - 1:HALO - 1 + TM, :] * cw[0:1, :]
            + gu_ref[HALO:HALO + TM, :] * cw[1:2, :]
            + gu_ref[HALO + 1:HALO + 1 + TM, :] * cw[2:3, :])
    bgate = _dot(xn_ref[HALO:HALO + TM, :], wb_ref[...])
    o_ref[...] += _dot((bgate * conv).astype(bf16), wo_ref[...])


def _conv_mixer(h, g, w_in, conv_w, w_out):
    R = h.shape[0]
    nc = D // TC
    hb = TM // HALO
    last = R // HALO - 1
    return pl.pallas_call(
        _conv_kernel,
        grid=(R // TM, nc),
        in_specs=[
            pl.BlockSpec((TM, D), lambda i, c: (i, 0)),
            pl.BlockSpec((HALO, D), lambda i, c: (jnp.maximum(i * hb - 1, 0), 0)),
            pl.BlockSpec((HALO, D), lambda i, c: (jnp.minimum((i + 1) * hb, last), 0)),
            pl.BlockSpec((1, D), lambda i, c: (0, 0)),
            pl.BlockSpec((D, TC), lambda i, c: (0, c)),
            pl.BlockSpec((D, TC), lambda i, c: (0, nc + c)),
            pl.BlockSpec((D, TC), lambda i, c: (0, 2 * nc + c)),
            pl.BlockSpec((3, TC), lambda i, c: (0, c)),
            pl.BlockSpec((TC, D), lambda i, c: (c, 0)),
        ],
        out_specs=pl.BlockSpec((TM, D), lambda i, c: (i, 0)),
        out_shape=jax.ShapeDtypeStruct((R, D), f32),
        scratch_shapes=[pltpu.VMEM((TM + 2 * HALO, D), bf16), pltpu.VMEM((TM + 2 * HALO, TC), f32)],
        compiler_params=_cparams(("parallel", "arbitrary")),
        name="gated_conv",
    )(h, h, h, g.reshape(1, D), w_in, w_in, w_in, conv_w, w_out)


def _dft_factors(L):
    best = None
    for l1 in range(16, L, 16):
        if L % l1:
            continue
        l2p = -(-(L // l1) // 16) * 16
        cost = (2 * 256 * -(-l1 // 256) + 4 * 256 * -(-l2p // 256), 2 * l1 + 4 * l2p)
        if best is None or cost < best[0]:
            best = (cost, l1, L // l1, l2p)
    return best[1:]


def _dft_tables(L, L1, L2, L2p):
    def cs(num, den):
        ang = (2.0 * math.pi / den) * (num % den).astype(f32)
        return jnp.cos(ang), jnp.sin(ang)

    i1 = jnp.arange(L1)
    c1, s1 = cs(i1[:, None] * i1[None, :], L1)
    w1 = jnp.concatenate([c1, -s1], axis=0).astype(bf16)
    n2 = jnp.arange(L2p)
    tc, ts = cs(n2[:, None] * i1[None, :], L)
    live = (n2 < L2)[:, None] * (L ** -0.5)
    tw = jnp.stack([tc * live, ts * live], axis=1)[..., None]
    c3, s3 = cs(n2[:, None] * n2[None, :], L2)
    inside = jnp.logical_and(n2[:, None] < L2, n2[None, :] < L2)
    w3 = jnp.stack([jnp.where(inside, c3, 0.0), jnp.where(inside, s3, 0.0)]).astype(bf16)
    ic = jnp.arange(FGROUP)
    cc, sc = cs(ic[:, None] * ic[None, :], FGROUP)
    wc = (jnp.stack([cc, sc]) * (FGROUP ** -0.5)).astype(bf16)
    return w1, tw, w3, wc


def _dft1_kernel(x_ref, w_ref, tw_ref, tr_ref, ti_ref, *, L1):
    t = _dot(w_ref[...], x_ref[...])
    tr, ti = t[:L1], t[L1:]
    tc, ts = tw_ref[0], tw_ref[1]
    tr_ref[...] = (tr * tc + ti * ts).astype(bf16)
    ti_ref[...] = (ti * tc - tr * ts).astype(bf16)


def _dft3_kernel(tr_ref, ti_ref, w3_ref, wc_ref, o_ref):
    c3, s3 = w3_ref[0], w3_ref[1]
    tr, ti = tr_ref[...], ti_ref[...]
    yr = (_dot(c3, tr) + _dot(s3, ti)).astype(bf16)
    yi = (_dot(c3, ti) - _dot(s3, tr)).astype(bf16)
    cc, sc = wc_ref[0], wc_ref[1]
    for gidx in range(D // FGROUP):
        sl = slice(gidx * FGROUP, (gidx + 1) * FGROUP)
        o_ref[:, sl] = (_dot(yr[:, sl], cc) + _dot(yi[:, sl], sc)).astype(bf16)


def _fourier_mix(hn, B, Lp):
    L = Lp - META0
    L1, L2, L2p = _dft_factors(L)
    w1, tw, w3, wc = _dft_tables(L, L1, L2, L2p)
    x = hn.reshape(B, Lp, D)[:, META0:].reshape(B, L1, L2 * D)
    t_shape = jax.ShapeDtypeStruct((B, L1, L2p * D), bf16)
    tr, ti = pl.pallas_call(
        functools.partial(_dft1_kernel, L1=L1),
        grid=(B, L2p),
        in_specs=[
            pl.BlockSpec((None, L1, D), lambda b, n: (b, 0, jnp.minimum(n, L2 - 1))),
            pl.BlockSpec((2 * L1, L1), lambda b, n: (0, 0)),
            pl.BlockSpec((None, 2, L1, 1), lambda b, n: (n, 0, 0, 0)),
        ],
        out_specs=[pl.BlockSpec((None, L1, D), lambda b, n: (b, 0, n))] * 2,
        out_shape=[t_shape, t_shape],
        compiler_params=_cparams(("parallel", "arbitrary")),
        name="seq_dft_stage1",
    )(x, w1, tw)
    tr = tr.reshape(B, L1, L2p, D)
    ti = ti.reshape(B, L1, L2p, D)
    mixed = pl.pallas_call(
        _dft3_kernel,
        grid=(B, L1),
        in_specs=[
            pl.BlockSpec((None, None, L2p, D), lambda b, k: (b, k, 0, 0)),
            pl.BlockSpec((None, None, L2p, D), lambda b, k: (b, k, 0, 0)),
            pl.BlockSpec((2, L2p, L2p), lambda b, k: (0, 0, 0)),
            pl.BlockSpec((2, FGROUP, FGROUP), lambda b, k: (0, 0, 0)),
        ],
        out_specs=pl.BlockSpec((None, L2p, D), lambda b, k: (b, 0, k)),
        out_shape=jax.ShapeDtypeStruct((B, L2p, L1 * D), bf16),
        compiler_params=_cparams(("parallel", "arbitrary")),
        name="seq_dft_stage2_channel_dft",
    )(tr, ti, w3, wc)
    mixed = mixed.reshape(B, L2p * L1, D)[:, :L]
    return jnp.pad(mixed, ((0, 0), (META0, 0), (0, 0))).reshape(B * Lp, D)


def _encode(x, meta, p):
    B, S, _ = x.shape
    Lp = S + FRONT
    front = jnp.concatenate([jnp.zeros((META0, D), f32), meta.astype(f32)], axis=0)
    h = jnp.concatenate([jnp.broadcast_to(front[None], (B, FRONT, D)), x], axis=1).reshape(B * Lp, D)

    qkv = _qkv(h, p["ln_mix"][0], *_rope_permute(p["a_wqkv"], p["a_q_norm"], p["a_k_norm"]),
               _rope_tables(S, B))
    h = _proj_residual(h, _flash(qkv, B, Lp), p["a_wo"])
    h = _ffn(h, p["ln_ffn"][0], p["ffn_w_in"][0], p["ffn_w_out"][0])
    h = _proj_residual(h, _fourier_mix(_norm(h, p["ln_mix"][1]), B, Lp), p["b_w"])
    h = _ffn(h, p["ln_ffn"][1], p["ffn_w_in"][1], p["ffn_w_out"][1])
    h = _conv_mixer(h, p["ln_mix"][2], p["c_w_in"], p["c_conv"], p["c_w_out"])
    h = _ffn(h, p["ln_ffn"][2], p["ffn_w_in"][2], p["ffn_w_out"][2])
    qkv = _qkv(h, p["ln_mix"][3], p["d_wqkv"], p["d_q_norm"], p["d_k_norm"], None)
    h = _proj_residual(h, _win_attention(qkv, p["d_sink"], B, Lp), p["d_wo"])
    h = _ffn(h, p["ln_ffn"][3], p["ffn_w_in"][3], p["ffn_w_out"][3])
    return h.reshape(B, Lp, D)[:, FRONT:]


def kernel(x_prompt, x_sample, meta_tokens, ln_mix, ln_ffn, a_wqkv, a_q_norm, a_k_norm, a_wo, b_w,
           c_w_in, c_conv, c_w_out, d_wqkv, d_q_norm, d_k_norm, d_sink, d_wo, ffn_w_in, ffn_w_out):
    p = {
        "ln_mix": ln_mix, "ln_ffn": ln_ffn,
        "a_wqkv": a_wqkv[0].astype(bf16), "a_q_norm": a_q_norm[0], "a_k_norm": a_k_norm[0],
        "a_wo": a_wo[0].astype(bf16), "b_w": b_w[0].astype(bf16),
        "c_w_in": c_w_in[0].astype(bf16), "c_conv": c_conv[0], "c_w_out": c_w_out[0].astype(bf16),
        "d_wqkv": d_wqkv[0].astype(bf16), "d_q_norm": d_q_norm[0], "d_k_norm": d_k_norm[0],
        "d_sink": d_sink[0], "d_wo": d_wo[0].astype(bf16),
        "ffn_w_in": [ffn_w_in[i].astype(bf16) for i in range(ffn_w_in.shape[0])],
        "ffn_w_out": [ffn_w_out[i].astype(bf16) for i in range(ffn_w_out.shape[0])],
    }
    return _encode(x_prompt, meta_tokens, p), _encode(x_sample, meta_tokens, p)
```

```python
import functools
import math

import numpy as np
import jax
import jax.numpy as jnp
from jax import lax
from jax.experimental import pallas as pl
from jax.experimental.pallas import tpu as pltpu

D = 2048
N_HEADS = 16
N_KV = 4
GROUP = N_HEADS // N_KV
DH = D // N_HEADS
QW = GROUP * DH
KV_DIM = N_KV * DH
QKV_DIM = D + 2 * KV_DIM
WINDOW = 128
ROPE_THETA = 10000.0
GRID_W = 64
FGROUP = 256
D_FF = 5632
N_META = 16
EPS = 1e-6
NEG = -1e30
LOG2E = 1.4426950408889634
Q_SCALE = DH ** -0.5 * LOG2E

FRONT = 256
META0 = FRONT - N_META
QB = 256
TM = 512
HALO = 16
SKEW = 1
VMEM_LIMIT = 52 * 1024 * 1024

bf16 = jnp.bfloat16
f32 = jnp.float32


def _cparams(sem):
    return pltpu.CompilerParams(dimension_semantics=sem, vmem_limit_bytes=VMEM_LIMIT)


def _rms(x, g):
    return x * lax.rsqrt(jnp.mean(x * x, axis=-1, keepdims=True) + EPS) * g


def _dot(a, b):
    return jnp.dot(a, b, preferred_element_type=f32)


def _dot_nt(a, b):
    return lax.dot_general(a, b, (((1,), (1,)), ((), ())), preferred_element_type=f32)


def _norm_kernel(h_ref, g_ref, o_ref):
    o_ref[...] = _rms(h_ref[...], g_ref[...]).astype(bf16)


def _norm(h, g):
    R = h.shape[0]
    return pl.pallas_call(
        _norm_kernel,
        grid=(R // TM,),
        in_specs=[pl.BlockSpec((TM, D), lambda i: (i, 0)), pl.BlockSpec((1, D), lambda i: (0, 0))],
        out_specs=pl.BlockSpec((TM, D), lambda i: (i, 0)),
        out_shape=jax.ShapeDtypeStruct((R, D), bf16),
        compiler_params=_cparams(("parallel",)),
        name="rms_norm",
    )(h, g.reshape(1, D))


def _head_norm(yh, gain):
    return yh * lax.rsqrt(jnp.mean(yh * yh, axis=-1, keepdims=True) + EPS) * gain


def _qkv_kernel(*refs, rope):
    if rope:
        h_ref, g_ref, w_ref, hg_ref, cos_ref, sin_ref, o_ref, xn_ref = refs
    else:
        h_ref, g_ref, w_ref, hg_ref, o_ref, xn_ref = refs
    xn_ref[...] = _rms(h_ref[...], g_ref[...]).astype(bf16)
    n_q = D // QW

    def project(j):
        return _dot(xn_ref[...], w_ref[:, j * QW:(j + 1) * QW])

    def finish(j, y):
        if j > n_q:
            o_ref[:, j * QW:(j + 1) * QW] = y.astype(bf16)
            return
        gain = hg_ref[0:1, :] if j < n_q else hg_ref[1:2, :]
        scale = Q_SCALE if j < n_q else 1.0
        for hh in range(GROUP):
            n = _head_norm(y[:, hh * DH:(hh + 1) * DH], gain)
            if rope:
                n = n * cos_ref[...] + pltpu.roll(n, DH // 2, 1) * sin_ref[...]
            o_ref[:, j * QW + hh * DH:j * QW + (hh + 1) * DH] = (n * scale).astype(bf16)

    n_tiles = QKV_DIM // QW
    y = project(0)
    for j in range(n_tiles):
        y_next = project(j + 1) if j + 1 < n_tiles else None
        finish(j, y)
        y = y_next


def _qkv(h, g, w, q_gain, k_gain, rope_tabs):
    R = h.shape[0]
    rope = rope_tabs is not None
    in_specs = [
        pl.BlockSpec((TM, D), lambda i: (i, 0)),
        pl.BlockSpec((1, D), lambda i: (0, 0)),
        pl.BlockSpec((D, QKV_DIM), lambda i: (0, 0)),
        pl.BlockSpec((2, DH), lambda i: (0, 0)),
    ]
    args = [h, g.reshape(1, D), w, jnp.stack([q_gain, k_gain])]
    if rope:
        in_specs += [pl.BlockSpec((TM, DH), lambda i: (i, 0))] * 2
        args += list(rope_tabs)
    return pl.pallas_call(
        functools.partial(_qkv_kernel, rope=rope),
        grid=(R // TM,),
        in_specs=in_specs,
        out_specs=pl.BlockSpec((TM, QKV_DIM), lambda i: (i, 0)),
        out_shape=jax.ShapeDtypeStruct((R, QKV_DIM), bf16),
        scratch_shapes=[pltpu.VMEM((TM, D), bf16)],
        compiler_params=_cparams(("parallel",)),
        name="qkv_proj",
    )(*args)


ROPE_PERM = np.concatenate([np.arange(0, 32), np.arange(64, 96), np.arange(32, 64), np.arange(96, 128)])


def _rope_permute(wqkv, q_gain, k_gain):
    n_qk = N_HEADS + N_KV
    cols = (np.arange(n_qk)[:, None] * DH + ROPE_PERM[None, :]).reshape(-1)
    cols = np.concatenate([cols, np.arange(n_qk * DH, QKV_DIM)])
    return wqkv[:, cols], q_gain[ROPE_PERM], k_gain[ROPE_PERM]


def _rope_tables(S, B):
    quarter = DH // 4
    inv = ROPE_THETA ** (-jnp.arange(quarter, dtype=f32) / quarter)
    t = jnp.arange(S)
    row = jnp.concatenate([jnp.zeros((FRONT,), f32), (t // GRID_W).astype(f32)])
    col = jnp.concatenate([jnp.zeros((FRONT,), f32), (t % GRID_W).astype(f32)])
    ang_r = row[:, None] * inv[None, :]
    ang_c = col[:, None] * inv[None, :]
    ang = jnp.concatenate([ang_r, ang_c, ang_r, ang_c], axis=1)
    cos, sin = jnp.cos(ang), jnp.sin(ang)
    sin = jnp.where((np.arange(DH) < DH // 2)[None, :], -sin, sin)
    return jnp.tile(cos, (B, 1)), jnp.tile(sin, (B, 1))


def _flash_kernel(q_ref, k_ref, v_ref, o_ref, m_ref, l_ref, acc_ref, *, n_chunks, tk):
    qi = pl.program_id(2)
    col0 = lax.broadcasted_iota(jnp.int32, (QB, FRONT), 1)
    def with_ones(v):
        return jnp.concatenate([v, jnp.ones(v.shape, bf16)], axis=1)

    k0 = k_ref[0:FRONT, :]
    v0 = with_ones(v_ref[0:FRONT, :])
    for hh in range(GROUP):
        s = jnp.where(col0 >= META0, _dot_nt(q_ref[:, hh * DH:(hh + 1) * DH], k0), NEG)
        m = jnp.max(s, axis=-1, keepdims=True)
        pv = _dot(jnp.exp2(s - m).astype(bf16), v0)
        m_ref[hh] = jnp.broadcast_to(m, (QB, DH))
        l_ref[hh] = pv[:, DH:]
        acc_ref[hh] = pv[:, :DH]

    def scores(hh, kc):
        return _dot_nt(q_ref[:, hh * DH:(hh + 1) * DH], kc)

    def update(hh, s, vc):
        m_prev = m_ref[hh]
        m_new = jnp.maximum(m_prev, jnp.max(s, axis=-1, keepdims=True))
        p = jnp.exp2(s - jnp.concatenate([m_new] * (tk // DH), axis=1))
        alpha = jnp.exp2(m_prev - m_new)
        pv = _dot(p.astype(bf16), vc)
        l_ref[hh] = alpha * l_ref[hh] + pv[:, DH:]
        acc_ref[hh] = alpha * acc_ref[hh] + pv[:, :DH]
        m_ref[hh] = m_new

    def chunk(kc, vc, pending, k_next):
        pending = list(pending)
        for hh in range(GROUP):
            ahead = hh + SKEW
            if ahead < GROUP:
                pending.append(scores(ahead, kc))
            elif k_next is not None:
                pending.append(scores(ahead - GROUP, k_next))
            update(hh, pending.pop(0), vc)
        return tuple(pending)

    def body(c, pending):
        start = pl.multiple_of(FRONT + c * tk, FRONT)
        return chunk(k_ref[pl.ds(start, tk), :], with_ones(v_ref[pl.ds(start, tk), :]), pending,
                     k_ref[pl.ds(start + tk, tk), :])

    k_first = k_ref[pl.ds(FRONT, tk), :]
    pending = lax.fori_loop(0, n_chunks - 1, body, tuple(scores(hh, k_first) for hh in range(SKEW)),
                            unroll=4)
    last = FRONT + (n_chunks - 1) * tk
    chunk(k_ref[pl.ds(last, tk), :], with_ones(v_ref[pl.ds(last, tk), :]), pending, None)
    rowi = lax.broadcasted_iota(jnp.int32, (QB, DH), 0)
    keep_rows = jnp.logical_or(qi > 0, rowi >= META0)
    for hh in range(GROUP):
        o_ref[:, hh * DH:(hh + 1) * DH] = jnp.where(keep_rows, acc_ref[hh] / l_ref[hh], 0.0).astype(bf16)


def _flash(qkv, B, Lp):
    R = qkv.shape[0]
    S = Lp - FRONT
    tk = 512 if S % 512 == 0 else 256
    nq = Lp // QB
    kcol, vcol = D // DH, (D + KV_DIM) // DH
    return pl.pallas_call(
        functools.partial(_flash_kernel, n_chunks=S // tk, tk=tk),
        grid=(B, N_KV, nq),
        in_specs=[
            pl.BlockSpec((QB, QW), lambda b, g, i: (b * nq + i, g)),
            pl.BlockSpec((Lp, DH), lambda b, g, i: (b, kcol + g)),
            pl.BlockSpec((Lp, DH), lambda b, g, i: (b, vcol + g)),
        ],
        out_specs=pl.BlockSpec((QB, QW), lambda b, g, i: (b * nq + i, g)),
        out_shape=jax.ShapeDtypeStruct((R, D), bf16),
        scratch_shapes=[pltpu.VMEM((GROUP, QB, DH), f32)] * 3,
        compiler_params=_cparams(("parallel", "parallel", "arbitrary")),
        name="dense_attention",
    )(qkv, qkv, qkv)


def _win_kernel(hs_ref, q_ref, k_ref, v_ref, o_ref, *, Lp):
    g = pl.program_id(1)
    qi = pl.program_id(2)
    band = 2 * QB
    start = pl.multiple_of(jnp.clip(QB * qi - WINDOW, 0, Lp - band), WINDOW)
    kb = k_ref[pl.ds(start, band), :]
    vb = v_ref[pl.ds(start, band), :]
    km = k_ref[FRONT - DH:FRONT, :]
    vm = v_ref[FRONT - DH:FRONT, :]
    meta_ok = lax.broadcasted_iota(jnp.int32, (QB, DH), 1) >= DH - N_META
    qpos = QB * qi + lax.broadcasted_iota(jnp.int32, (QB, band), 0)
    kpos = start + lax.broadcasted_iota(jnp.int32, (QB, band), 1)
    dist = jnp.abs(qpos - kpos)
    ok = jnp.logical_and(dist <= WINDOW, kpos >= FRONT)
    distf = dist.astype(f32)
    rowi = lax.broadcasted_iota(jnp.int32, (QB, DH), 0)
    keep_rows = jnp.logical_or(qi > 0, rowi >= META0)
    def scores(hh):
        q = q_ref[:, hh * DH:(hh + 1) * DH]
        return _dot_nt(q, kb), _dot_nt(q, km)

    nxt = scores(0)
    for hh in range(GROUP):
        raw_b, raw_m = nxt
        if hh + 1 < GROUP:
            nxt = scores(hh + 1)
        slope = jnp.where(qi == 0, 0.0, hs_ref[0, g * GROUP + hh])
        sink = hs_ref[1, g * GROUP + hh]
        sb = jnp.where(ok, raw_b - slope * distf, NEG)
        sm = jnp.where(meta_ok, raw_m, NEG)
        m = jnp.maximum(jnp.max(sb, axis=-1, keepdims=True), jnp.max(sm, axis=-1, keepdims=True))
        m = jnp.maximum(m, sink)
        eb = jnp.exp2(sb - m)
        em = jnp.exp2(sm - m)
        den = (jnp.sum(eb, axis=-1, keepdims=True) + jnp.sum(em, axis=-1, keepdims=True)
               + jnp.exp2(sink - m))
        acc = _dot(eb.astype(bf16), vb) + _dot(em.astype(bf16), vm)
        o_ref[:, hh * DH:(hh + 1) * DH] = jnp.where(keep_rows, acc / den, 0.0).astype(bf16)


def _win_attention(qkv, sink, B, Lp):
    R = qkv.shape[0]
    nq = Lp // QB
    kcol, vcol = D // DH, (D + KV_DIM) // DH
    slopes = 2.0 ** (-8.0 * jnp.arange(1, N_HEADS + 1, dtype=f32) / N_HEADS)
    hs = jnp.stack([slopes, sink.astype(f32)]) * LOG2E
    return pl.pallas_call(
        functools.partial(_win_kernel, Lp=Lp),
        grid=(B, N_KV, nq),
        in_specs=[
            pl.BlockSpec(memory_space=pltpu.SMEM),
            pl.BlockSpec((QB, QW), lambda b, g, i: (b * nq + i, g)),
            pl.BlockSpec((Lp, DH), lambda b, g, i: (b, kcol + g)),
            pl.BlockSpec((Lp, DH), lambda b, g, i: (b, vcol + g)),
        ],
        out_specs=pl.BlockSpec((QB, QW), lambda b, g, i: (b * nq + i, g)),
        out_shape=jax.ShapeDtypeStruct((R, D), bf16),
        compiler_params=_cparams(("parallel", "parallel", "arbitrary")),
        name="window_attention",
    )(hs, qkv, qkv, qkv)


TF = 512


def _ffn_kernel(*refs, mixed):
    if mixed:
        h_ref, a_ref, wp_ref, g_ref, wg_ref, wu_ref, wo_ref, o_ref, xn_ref = refs
    else:
        h_ref, g_ref, wg_ref, wu_ref, wo_ref, o_ref, xn_ref = refs

    @pl.when(pl.program_id(1) == 0)
    def _():
        x = h_ref[...]
        if mixed:
            x = x + _dot(a_ref[...], wp_ref[...])
        xn_ref[...] = _rms(x, g_ref[...]).astype(bf16)
        o_ref[...] = x

    xn = xn_ref[...]
    gate = _dot(xn, wg_ref[...])
    up = _dot(xn, wu_ref[...])
    a = gate * jax.nn.sigmoid(gate) * up
    o_ref[...] += _dot(a.astype(bf16), wo_ref[...])


def _ffn(h, g, w_in, w_out, mix=None):
    R = h.shape[0]
    nf = D_FF // TF
    row = pl.BlockSpec((TM, D), lambda i, f: (i, 0))
    in_specs, args = [row], [h]
    if mix is not None:
        in_specs += [row, pl.BlockSpec((D, D), lambda i, f: (0, 0), pipeline_mode=pl.Buffered(1))]
        args += list(mix)
    in_specs += [
        pl.BlockSpec((1, D), lambda i, f: (0, 0)),
        pl.BlockSpec((D, TF), lambda i, f: (0, f)),
        pl.BlockSpec((D, TF), lambda i, f: (0, nf + f)),
        pl.BlockSpec((TF, D), lambda i, f: (f, 0)),
    ]
    args += [g.reshape(1, D), w_in, w_in, w_out]
    return pl.pallas_call(
        functools.partial(_ffn_kernel, mixed=mix is not None),
        grid=(R // TM, nf),
        in_specs=in_specs,
        out_specs=row,
        out_shape=jax.ShapeDtypeStruct((R, D), f32),
        scratch_shapes=[pltpu.VMEM((TM, D), bf16)],
        compiler_params=_cparams(("parallel", "arbitrary")),
        name="swiglu_ffn",
    )(*args)


TC = 512


def _conv_kernel(h_ref, hp_ref, hn_ref, g_ref, wb_ref, wc_ref, wu_ref, cw_ref, wo_ref, o_ref,
                 xn_ref, gu_ref):
    i = pl.program_id(0)

    @pl.when(pl.program_id(1) == 0)
    def _():
        x = h_ref[...]
        gain = g_ref[...]
        xn_ref[HALO:HALO + TM, :] = _rms(x, gain).astype(bf16)
        xn_ref[0:HALO, :] = _rms(hp_ref[...], gain).astype(bf16)
        nxt = _rms(hn_ref[...], gain)
        nxt = jnp.where(i == pl.num_programs(0) - 1, 0.0, nxt)
        xn_ref[HALO + TM:, :] = nxt.astype(bf16)
        o_ref[...] = x

    xn = xn_ref[...]
    gu_ref[...] = _dot(xn, wc_ref[...]) * _dot(xn, wu_ref[...])
    cw = cw_ref[...]
    conv = (gu_ref[HALO - 1:HALO - 1 + TM, :] * cw[0:1, :]
            + gu_ref[HALO:HALO + TM, :] * cw[1:2, :]
            + gu_ref[HALO + 1:HALO + 1 + TM, :] * cw[2:3, :])
    bgate = _dot(xn_ref[HALO:HALO + TM, :], wb_ref[...])
    o_ref[...] += _dot((bgate * conv).astype(bf16), wo_ref[...])


def _conv_mixer(h, g, w_in, conv_w, w_out):
    R = h.shape[0]
    nc = D // TC
    hb = TM // HALO
    last = R // HALO - 1
    return pl.pallas_call(
        _conv_kernel,
        grid=(R // TM, nc),
        in_specs=[
            pl.BlockSpec((TM, D), lambda i, c: (i, 0)),
            pl.BlockSpec((HALO, D), lambda i, c: (jnp.maximum(i * hb - 1, 0), 0)),
            pl.BlockSpec((HALO, D), lambda i, c: (jnp.minimum((i + 1) * hb, last), 0)),
            pl.BlockSpec((1, D), lambda i, c: (0, 0)),
            pl.BlockSpec((D, TC), lambda i, c: (0, c)),
            pl.BlockSpec((D, TC), lambda i, c: (0, nc + c)),
            pl.BlockSpec((D, TC), lambda i, c: (0, 2 * nc + c)),
            pl.BlockSpec((3, TC), lambda i, c: (0, c)),
            pl.BlockSpec((TC, D), lambda i, c: (c, 0)),
        ],
        out_specs=pl.BlockSpec((TM, D), lambda i, c: (i, 0)),
        out_shape=jax.ShapeDtypeStruct((R, D), f32),
        scratch_shapes=[pltpu.VMEM((TM + 2 * HALO, D), bf16), pltpu.VMEM((TM + 2 * HALO, TC), f32)],
        compiler_params=_cparams(("parallel", "arbitrary")),
        name="gated_conv",
    )(h, h, h, g.reshape(1, D), w_in, w_in, w_in, conv_w, w_out)


def _dft_factors(L):
    best = None
    for l1 in range(16, L, 16):
        if L % l1:
            continue
        l2p = -(-(L // l1) // 16) * 16
        cost = (2 * 256 * -(-l1 // 256) + 4 * 256 * -(-l2p // 256), 2 * l1 + 4 * l2p)
        if best is None or cost < best[0]:
            best = (cost, l1, L // l1, l2p)
    return best[1:]


def _dft_tables(L, L1, L2, L2p):
    def cs(num, den):
        ang = (2.0 * math.pi / den) * (num % den).astype(f32)
        return jnp.cos(ang), jnp.sin(ang)

    i1 = jnp.arange(L1)
    c1, s1 = cs(i1[:, None] * i1[None, :], L1)
    w1 = jnp.concatenate([c1, -s1], axis=0).astype(bf16)
    n2 = jnp.arange(L2p)
    tc, ts = cs(n2[:, None] * i1[None, :], L)
    live = (n2 < L2)[:, None] * (L ** -0.5)
    tw = jnp.stack([tc * live, ts * live], axis=1)[..., None]
    c3, s3 = cs(n2[:, None] * n2[None, :], L2)
    inside = jnp.logical_and(n2[:, None] < L2, n2[None, :] < L2)
    w3 = jnp.stack([jnp.where(inside, c3, 0.0), jnp.where(inside, s3, 0.0)]).astype(bf16)
    ic = jnp.arange(FGROUP)
    cc, sc = cs(ic[:, None] * ic[None, :], FGROUP)
    wc = (jnp.stack([cc, sc]) * (FGROUP ** -0.5)).astype(bf16)
    return w1, tw, w3, wc


def _dft1_kernel(x_ref, w_ref, tw_ref, tr_ref, ti_ref, *, L1):
    t = _dot(w_ref[...], x_ref[...])
    tr, ti = t[:L1], t[L1:]
    tc, ts = tw_ref[0], tw_ref[1]
    tr_ref[...] = (tr * tc + ti * ts).astype(bf16)
    ti_ref[...] = (ti * tc - tr * ts).astype(bf16)


def _dft3_kernel(tr_ref, ti_ref, w3_ref, wc_ref, o_ref):
    c3, s3 = w3_ref[0], w3_ref[1]
    tr, ti = tr_ref[...], ti_ref[...]
    yr = (_dot(c3, tr) + _dot(s3, ti)).astype(bf16)
    yi = (_dot(c3, ti) - _dot(s3, tr)).astype(bf16)
    cc, sc = wc_ref[0], wc_ref[1]
    for gidx in range(D // FGROUP):
        sl = slice(gidx * FGROUP, (gidx + 1) * FGROUP)
        o_ref[:, sl] = (_dot(yr[:, sl], cc) + _dot(yi[:, sl], sc)).astype(bf16)


def _fourier_mix(hn, B, Lp):
    L = Lp - META0
    L1, L2, L2p = _dft_factors(L)
    w1, tw, w3, wc = _dft_tables(L, L1, L2, L2p)
    x = hn.reshape(B, Lp, D)[:, META0:].reshape(B, L1, L2 * D)
    t_shape = jax.ShapeDtypeStruct((B, L1, L2p * D), bf16)
    tr, ti = pl.pallas_call(
        functools.partial(_dft1_kernel, L1=L1),
        grid=(B, L2p),
        in_specs=[
            pl.BlockSpec((None, L1, D), lambda b, n: (b, 0, jnp.minimum(n, L2 - 1))),
            pl.BlockSpec((2 * L1, L1), lambda b, n: (0, 0)),
            pl.BlockSpec((None, 2, L1, 1), lambda b, n: (n, 0, 0, 0)),
        ],
        out_specs=[pl.BlockSpec((None, L1, D), lambda b, n: (b, 0, n))] * 2,
        out_shape=[t_shape, t_shape],
        compiler_params=_cparams(("parallel", "arbitrary")),
        name="seq_dft_stage1",
    )(x, w1, tw)
    tr = tr.reshape(B, L1, L2p, D)
    ti = ti.reshape(B, L1, L2p, D)
    mixed = pl.pallas_call(
        _dft3_kernel,
        grid=(B, L1),
        in_specs=[
            pl.BlockSpec((None, None, L2p, D), lambda b, k: (b, k, 0, 0)),
            pl.BlockSpec((None, None, L2p, D), lambda b, k: (b, k, 0, 0)),
            pl.BlockSpec((2, L2p, L2p), lambda b, k: (0, 0, 0)),
            pl.BlockSpec((2, FGROUP, FGROUP), lambda b, k: (0, 0, 0)),
        ],
        out_specs=pl.BlockSpec((None, L2p, D), lambda b, k: (b, 0, k)),
        out_shape=jax.ShapeDtypeStruct((B, L2p, L1 * D), bf16),
        compiler_params=_cparams(("parallel", "arbitrary")),
        name="seq_dft_stage2_channel_dft",
    )(tr, ti, w3, wc)
    mixed = mixed.reshape(B, L2p * L1, D)[:, :L]
    return jnp.pad(mixed, ((0, 0), (META0, 0), (0, 0))).reshape(B * Lp, D)


def _encode(x, meta, p):
    B, S, _ = x.shape
    Lp = S + FRONT
    front = jnp.concatenate([jnp.zeros((META0, D), f32), meta.astype(f32)], axis=0)
    h = jnp.concatenate([jnp.broadcast_to(front[None], (B, FRONT, D)), x], axis=1).reshape(B * Lp, D)

    qkv = _qkv(h, p["ln_mix"][0], *_rope_permute(p["a_wqkv"], p["a_q_norm"], p["a_k_norm"]),
               _rope_tables(S, B))
    h = _ffn(h, p["ln_ffn"][0], p["ffn_w_in"][0], p["ffn_w_out"][0], (_flash(qkv, B, Lp), p["a_wo"]))
    h = _ffn(h, p["ln_ffn"][1], p["ffn_w_in"][1], p["ffn_w_out"][1],
             (_fourier_mix(_norm(h, p["ln_mix"][1]), B, Lp), p["b_w"]))
    h = _conv_mixer(h, p["ln_mix"][2], p["c_w_in"], p["c_conv"], p["c_w_out"])
    h = _ffn(h, p["ln_ffn"][2], p["ffn_w_in"][2], p["ffn_w_out"][2])
    qkv = _qkv(h, p["ln_mix"][3], p["d_wqkv"], p["d_q_norm"], p["d_k_norm"], None)
    h = _ffn(h, p["ln_ffn"][3], p["ffn_w_in"][3], p["ffn_w_out"][3],
             (_win_attention(qkv, p["d_sink"], B, Lp), p["d_wo"]))
    return h.reshape(B, Lp, D)[:, FRONT:]


def kernel(x_prompt, x_sample, meta_tokens, ln_mix, ln_ffn, a_wqkv, a_q_norm, a_k_norm, a_wo, b_w,
           c_w_in, c_conv, c_w_out, d_wqkv, d_q_norm, d_k_norm, d_sink, d_wo, ffn_w_in, ffn_w_out):
    p = {
        "ln_mix": ln_mix, "ln_ffn": ln_ffn,
        "a_wqkv": a_wqkv[0].astype(bf16), "a_q_norm": a_q_norm[0], "a_k_norm": a_k_norm[0],
        "a_wo": a_wo[0].astype(bf16), "b_w": b_w[0].astype(bf16),
        "c_w_in": c_w_in[0].astype(bf16), "c_conv": c_conv[0], "c_w_out": c_w_out[0].astype(bf16),
        "d_wqkv": d_wqkv[0].astype(bf16), "d_q_norm": d_q_norm[0], "d_k_norm": d_k_norm[0],
        "d_sink": d_sink[0], "d_wo": d_wo[0].astype(bf16),
        "ffn_w_in": [ffn_w_in[i].astype(bf16) for i in range(ffn_w_in.shape[0])],
        "ffn_w_out": [ffn_w_out[i].astype(bf16) for i in range(ffn_w_out.shape[0])],
    }
    return _encode(x_prompt, meta_tokens, p), _encode(x_sample, meta_tokens, p)
```

```python
import functools
import math

import numpy as np
import jax
import jax.numpy as jnp
from jax import lax
from jax.experimental import pallas as pl
from jax.experimental.pallas import tpu as pltpu

D = 2048
N_HEADS = 16
N_KV = 4
GROUP = N_HEADS // N_KV
DH = D // N_HEADS
QW = GROUP * DH
KV_DIM = N_KV * DH
QKV_DIM = D + 2 * KV_DIM
WINDOW = 128
ROPE_THETA = 10000.0
GRID_W = 64
FGROUP = 256
D_FF = 5632
N_META = 16
EPS = 1e-6
NEG = -1e30
LOG2E = 1.4426950408889634
Q_SCALE = DH ** -0.5 * LOG2E

FRONT = 256
META0 = FRONT - N_META
QB = 256
TM = 512
HALO = 16
SKEW = 1
VMEM_LIMIT = 52 * 1024 * 1024

bf16 = jnp.bfloat16
f32 = jnp.float32


def _cparams(sem):
    return pltpu.CompilerParams(dimension_semantics=sem, vmem_limit_bytes=VMEM_LIMIT)


def _rms(x, g):
    return x * lax.rsqrt(jnp.mean(x * x, axis=-1, keepdims=True) + EPS) * g


def _dot(a, b):
    return jnp.dot(a, b, preferred_element_type=f32)


def _dot_nt(a, b):
    return lax.dot_general(a, b, (((1,), (1,)), ((), ())), preferred_element_type=f32)


def _norm_kernel(h_ref, g_ref, o_ref):
    o_ref[...] = _rms(h_ref[...], g_ref[...]).astype(bf16)


def _norm(h, g):
    R = h.shape[0]
    return pl.pallas_call(
        _norm_kernel,
        grid=(R // TM,),
        in_specs=[pl.BlockSpec((TM, D), lambda i: (i, 0)), pl.BlockSpec((1, D), lambda i: (0, 0))],
        out_specs=pl.BlockSpec((TM, D), lambda i: (i, 0)),
        out_shape=jax.ShapeDtypeStruct((R, D), bf16),
        compiler_params=_cparams(("parallel",)),
        name="rms_norm",
    )(h, g.reshape(1, D))


def _head_norm(yh, gain):
    return yh * lax.rsqrt(jnp.mean(yh * yh, axis=-1, keepdims=True) + EPS) * gain


def _qkv_kernel(*refs, rope):
    if rope:
        h_ref, g_ref, w_ref, hg_ref, cos_ref, sin_ref, o_ref, xn_ref = refs
    else:
        h_ref, g_ref, w_ref, hg_ref, o_ref, xn_ref = refs
    xn_ref[...] = _rms(h_ref[...], g_ref[...]).astype(bf16)
    n_q = D // QW

    def project(j):
        return _dot(xn_ref[...], w_ref[:, j * QW:(j + 1) * QW])

    def finish(j, y):
        if j > n_q:
            o_ref[:, j * QW:(j + 1) * QW] = y.astype(bf16)
            return
        gain = hg_ref[0:1, :] if j < n_q else hg_ref[1:2, :]
        scale = Q_SCALE if j < n_q else 1.0
        for hh in range(GROUP):
            n = _head_norm(y[:, hh * DH:(hh + 1) * DH], gain)
            if rope:
                n = n * cos_ref[...] + pltpu.roll(n, DH // 2, 1) * sin_ref[...]
            o_ref[:, j * QW + hh * DH:j * QW + (hh + 1) * DH] = (n * scale).astype(bf16)

    n_tiles = QKV_DIM // QW
    y = project(0)
    for j in range(n_tiles):
        y_next = project(j + 1) if j + 1 < n_tiles else None
        finish(j, y)
        y = y_next


def _qkv(h, g, w, q_gain, k_gain, rope_tabs):
    R = h.shape[0]
    rope = rope_tabs is not None
    in_specs = [
        pl.BlockSpec((TM, D), lambda i: (i, 0)),
        pl.BlockSpec((1, D), lambda i: (0, 0)),
        pl.BlockSpec((D, QKV_DIM), lambda i: (0, 0)),
        pl.BlockSpec((2, DH), lambda i: (0, 0)),
    ]
    args = [h, g.reshape(1, D), w, jnp.stack([q_gain, k_gain])]
    if rope:
        in_specs += [pl.BlockSpec((TM, DH), lambda i: (i, 0))] * 2
        args += list(rope_tabs)
    return pl.pallas_call(
        functools.partial(_qkv_kernel, rope=rope),
        grid=(R // TM,),
        in_specs=in_specs,
        out_specs=pl.BlockSpec((TM, QKV_DIM), lambda i: (i, 0)),
        out_shape=jax.ShapeDtypeStruct((R, QKV_DIM), bf16),
        scratch_shapes=[pltpu.VMEM((TM, D), bf16)],
        compiler_params=_cparams(("parallel",)),
        name="qkv_proj",
    )(*args)


ROPE_PERM = np.concatenate([np.arange(0, 32), np.arange(64, 96), np.arange(32, 64), np.arange(96, 128)])


def _rope_permute(wqkv, q_gain, k_gain):
    n_qk = N_HEADS + N_KV
    cols = (np.arange(n_qk)[:, None] * DH + ROPE_PERM[None, :]).reshape(-1)
    cols = np.concatenate([cols, np.arange(n_qk * DH, QKV_DIM)])
    return wqkv[:, cols], q_gain[ROPE_PERM], k_gain[ROPE_PERM]


def _rope_tables(S, B):
    quarter = DH // 4
    inv = ROPE_THETA ** (-jnp.arange(quarter, dtype=f32) / quarter)
    t = jnp.arange(S)
    row = jnp.concatenate([jnp.zeros((FRONT,), f32), (t // GRID_W).astype(f32)])
    col = jnp.concatenate([jnp.zeros((FRONT,), f32), (t % GRID_W).astype(f32)])
    ang_r = row[:, None] * inv[None, :]
    ang_c = col[:, None] * inv[None, :]
    ang = jnp.concatenate([ang_r, ang_c, ang_r, ang_c], axis=1)
    cos, sin = jnp.cos(ang), jnp.sin(ang)
    sin = jnp.where((np.arange(DH) < DH // 2)[None, :], -sin, sin)
    return jnp.tile(cos, (B, 1)), jnp.tile(sin, (B, 1))


def _flash_kernel(q_ref, k_ref, v_ref, o_ref, m_ref, l_ref, acc_ref, *, n_chunks, tk):
    qi = pl.program_id(2)
    col0 = lax.broadcasted_iota(jnp.int32, (QB, DH), 1) + (FRONT - DH)

    def with_ones(v):
        return jnp.concatenate([v, jnp.ones(v.shape, bf16)], axis=1)

    k0 = k_ref[FRONT - DH:FRONT, :]
    v0 = with_ones(v_ref[FRONT - DH:FRONT, :])
    for hh in range(GROUP):
        s = jnp.where(col0 >= META0, _dot_nt(q_ref[:, hh * DH:(hh + 1) * DH], k0), NEG)
        m = jnp.max(s, axis=-1, keepdims=True)
        pv = _dot(jnp.exp2(s - m).astype(bf16), v0)
        m_ref[hh] = jnp.broadcast_to(m, (QB, DH))
        l_ref[hh] = pv[:, DH:]
        acc_ref[hh] = pv[:, :DH]

    def scores(hh, kc):
        return _dot_nt(q_ref[:, hh * DH:(hh + 1) * DH], kc)

    def update(hh, s, vc):
        m_prev = m_ref[hh]
        m_new = jnp.maximum(m_prev, jnp.max(s, axis=-1, keepdims=True))
        p = jnp.exp2((s - jnp.concatenate([m_new] * (tk // DH), axis=1)).astype(bf16))
        alpha = jnp.exp2(m_prev - m_new)
        pv = _dot(p, vc)
        l_ref[hh] = alpha * l_ref[hh] + pv[:, DH:]
        acc_ref[hh] = alpha * acc_ref[hh] + pv[:, :DH]
        m_ref[hh] = m_new

    def chunk(kc, vc, pending, k_next):
        pending = list(pending)
        for hh in range(GROUP):
            ahead = hh + SKEW
            if ahead < GROUP:
                pending.append(scores(ahead, kc))
            elif k_next is not None:
                pending.append(scores(ahead - GROUP, k_next))
            update(hh, pending.pop(0), vc)
        return tuple(pending)

    def body(c, pending):
        start = pl.multiple_of(FRONT + c * tk, FRONT)
        return chunk(k_ref[pl.ds(start, tk), :], with_ones(v_ref[pl.ds(start, tk), :]), pending,
                     k_ref[pl.ds(start + tk, tk), :])

    k_first = k_ref[pl.ds(FRONT, tk), :]
    pending = lax.fori_loop(0, n_chunks - 1, body, tuple(scores(hh, k_first) for hh in range(SKEW)),
                            unroll=4)
    last = FRONT + (n_chunks - 1) * tk
    chunk(k_ref[pl.ds(last, tk), :], with_ones(v_ref[pl.ds(last, tk), :]), pending, None)
    rowi = lax.broadcasted_iota(jnp.int32, (QB, DH), 0)
    keep_rows = jnp.logical_or(qi > 0, rowi >= META0)
    for hh in range(GROUP):
        o_ref[:, hh * DH:(hh + 1) * DH] = jnp.where(keep_rows, acc_ref[hh] / l_ref[hh], 0.0).astype(bf16)


def _flash(qkv, B, Lp):
    R = qkv.shape[0]
    S = Lp - FRONT
    tk = 512 if S % 512 == 0 else 256
    nq = Lp // QB
    kcol, vcol = D // DH, (D + KV_DIM) // DH
    return pl.pallas_call(
        functools.partial(_flash_kernel, n_chunks=S // tk, tk=tk),
        grid=(B, N_KV, nq),
        in_specs=[
            pl.BlockSpec((QB, QW), lambda b, g, i: (b * nq + i, g)),
            pl.BlockSpec((Lp, DH), lambda b, g, i: (b, kcol + g)),
            pl.BlockSpec((Lp, DH), lambda b, g, i: (b, vcol + g)),
        ],
        out_specs=pl.BlockSpec((QB, QW), lambda b, g, i: (b * nq + i, g)),
        out_shape=jax.ShapeDtypeStruct((R, D), bf16),
        scratch_shapes=[pltpu.VMEM((GROUP, QB, DH), f32)] * 3,
        compiler_params=_cparams(("parallel", "parallel", "arbitrary")),
        name="dense_attention",
    )(qkv, qkv, qkv)


def _win_kernel(hs_ref, q_ref, k_ref, v_ref, o_ref, *, Lp):
    g = pl.program_id(1)
    qi = pl.program_id(2)
    band = 2 * QB
    start = pl.multiple_of(jnp.clip(QB * qi - WINDOW, 0, Lp - band), WINDOW)
    kb = k_ref[pl.ds(start, band), :]
    vb = v_ref[pl.ds(start, band), :]
    km = k_ref[FRONT - DH:FRONT, :]
    vm = v_ref[FRONT - DH:FRONT, :]
    meta_ok = lax.broadcasted_iota(jnp.int32, (QB, DH), 1) >= DH - N_META
    qpos = QB * qi + lax.broadcasted_iota(jnp.int32, (QB, band), 0)
    kpos = start + lax.broadcasted_iota(jnp.int32, (QB, band), 1)
    dist = jnp.abs(qpos - kpos)
    ok = jnp.logical_and(dist <= WINDOW, kpos >= FRONT)
    distf = dist.astype(f32)
    rowi = lax.broadcasted_iota(jnp.int32, (QB, DH), 0)
    keep_rows = jnp.logical_or(qi > 0, rowi >= META0)
    def scores(hh):
        q = q_ref[:, hh * DH:(hh + 1) * DH]
        return _dot_nt(q, kb), _dot_nt(q, km)

    nxt = scores(0)
    for hh in range(GROUP):
        raw_b, raw_m = nxt
        if hh + 1 < GROUP:
            nxt = scores(hh + 1)
        slope = jnp.where(qi == 0, 0.0, hs_ref[0, g * GROUP + hh])
        sink = hs_ref[1, g * GROUP + hh]
        sb = jnp.where(ok, raw_b - slope * distf, NEG)
        sm = jnp.where(meta_ok, raw_m, NEG)
        m = jnp.maximum(jnp.max(sb, axis=-1, keepdims=True), jnp.max(sm, axis=-1, keepdims=True))
        m = jnp.maximum(m, sink)
        eb = jnp.exp2(sb - m)
        em = jnp.exp2(sm - m)
        den = (jnp.sum(eb, axis=-1, keepdims=True) + jnp.sum(em, axis=-1, keepdims=True)
               + jnp.exp2(sink - m))
        acc = _dot(eb.astype(bf16), vb) + _dot(em.astype(bf16), vm)
        o_ref[:, hh * DH:(hh + 1) * DH] = jnp.where(keep_rows, acc / den, 0.0).astype(bf16)


def _win_attention(qkv, sink, B, Lp):
    R = qkv.shape[0]
    nq = Lp // QB
    kcol, vcol = D // DH, (D + KV_DIM) // DH
    slopes = 2.0 ** (-8.0 * jnp.arange(1, N_HEADS + 1, dtype=f32) / N_HEADS)
    hs = jnp.stack([slopes, sink.astype(f32)]) * LOG2E
    return pl.pallas_call(
        functools.partial(_win_kernel, Lp=Lp),
        grid=(B, N_KV, nq),
        in_specs=[
            pl.BlockSpec(memory_space=pltpu.SMEM),
            pl.BlockSpec((QB, QW), lambda b, g, i: (b * nq + i, g)),
            pl.BlockSpec((Lp, DH), lambda b, g, i: (b, kcol + g)),
            pl.BlockSpec((Lp, DH), lambda b, g, i: (b, vcol + g)),
        ],
        out_specs=pl.BlockSpec((QB, QW), lambda b, g, i: (b * nq + i, g)),
        out_shape=jax.ShapeDtypeStruct((R, D), bf16),
        compiler_params=_cparams(("parallel", "parallel", "arbitrary")),
        name="window_attention",
    )(hs, qkv, qkv, qkv)


TF = 512


def _ffn_kernel(*refs, mixed):
    if mixed:
        h_ref, a_ref, wp_ref, g_ref, wg_ref, wu_ref, wo_ref, o_ref, xn_ref = refs
    else:
        h_ref, g_ref, wg_ref, wu_ref, wo_ref, o_ref, xn_ref = refs

    @pl.when(pl.program_id(1) == 0)
    def _():
        x = h_ref[...]
        if mixed:
            x = x + _dot(a_ref[...], wp_ref[...])
        xn_ref[...] = _rms(x, g_ref[...]).astype(bf16)
        o_ref[...] = x

    xn = xn_ref[...]
    gate = _dot(xn, wg_ref[...])
    up = _dot(xn, wu_ref[...])
    a = gate * jax.nn.sigmoid(gate) * up
    o_ref[...] += _dot(a.astype(bf16), wo_ref[...])


def _ffn(h, g, w_in, w_out, mix=None):
    R = h.shape[0]
    nf = D_FF // TF
    row = pl.BlockSpec((TM, D), lambda i, f: (i, 0))
    in_specs, args = [row], [h]
    if mix is not None:
        in_specs += [row, pl.BlockSpec((D, D), lambda i, f: (0, 0), pipeline_mode=pl.Buffered(1))]
        args += list(mix)
    in_specs += [
        pl.BlockSpec((1, D), lambda i, f: (0, 0)),
        pl.BlockSpec((D, TF), lambda i, f: (0, f)),
        pl.BlockSpec((D, TF), lambda i, f: (0, nf + f)),
        pl.BlockSpec((TF, D), lambda i, f: (f, 0)),
    ]
    args += [g.reshape(1, D), w_in, w_in, w_out]
    return pl.pallas_call(
        functools.partial(_ffn_kernel, mixed=mix is not None),
        grid=(R // TM, nf),
        in_specs=in_specs,
        out_specs=row,
        out_shape=jax.ShapeDtypeStruct((R, D), f32),
        scratch_shapes=[pltpu.VMEM((TM, D), bf16)],
        compiler_params=_cparams(("parallel", "arbitrary")),
        name="swiglu_ffn",
    )(*args)


TC = 512


def _conv_kernel(h_ref, hp_ref, hn_ref, g_ref, wb_ref, wc_ref, wu_ref, cw_ref, wo_ref, o_ref,
                 xn_ref, gu_ref):
    i = pl.program_id(0)

    @pl.when(pl.program_id(1) == 0)
    def _():
        x = h_ref[...]
        gain = g_ref[...]
        xn_ref[HALO:HALO + TM, :] = _rms(x, gain).astype(bf16)
        xn_ref[0:HALO, :] = _rms(hp_ref[...], gain).astype(bf16)
        nxt = _rms(hn_ref[...], gain)
        nxt = jnp.where(i == pl.num_programs(0) - 1, 0.0, nxt)
        xn_ref[HALO + TM:, :] = nxt.astype(bf16)
        o_ref[...] = x

    xn = xn_ref[...]
    gu_ref[...] = _dot(xn, wc_ref[...]) * _dot(xn, wu_ref[...])
    cw = cw_ref[...]
    conv = (gu_ref[HALO - 1:HALO - 1 + TM, :] * cw[0:1, :]
            + gu_ref[HALO:HALO + TM, :] * cw[1:2, :]
            + gu_ref[HALO + 1:HALO + 1 + TM, :] * cw[2:3, :])
    bgate = _dot(xn_ref[HALO:HALO + TM, :], wb_ref[...])
    o_ref[...] += _dot((bgate * conv).astype(bf16), wo_ref[...])


def _conv_mixer(h, g, w_in, conv_w, w_out):
    R = h.shape[0]
    nc = D // TC
    hb = TM // HALO
    last = R // HALO - 1
    return pl.pallas_call(
        _conv_kernel,
        grid=(R // TM, nc),
        in_specs=[
            pl.BlockSpec((TM, D), lambda i, c: (i, 0)),
            pl.BlockSpec((HALO, D), lambda i, c: (jnp.maximum(i * hb - 1, 0), 0)),
            pl.BlockSpec((HALO, D), lambda i, c: (jnp.minimum((i + 1) * hb, last), 0)),
            pl.BlockSpec((1, D), lambda i, c: (0, 0)),
            pl.BlockSpec((D, TC), lambda i, c: (0, c)),
            pl.BlockSpec((D, TC), lambda i, c: (0, nc + c)),
            pl.BlockSpec((D, TC), lambda i, c: (0, 2 * nc + c)),
            pl.BlockSpec((3, TC), lambda i, c: (0, c)),
            pl.BlockSpec((TC, D), lambda i, c: (c, 0)),
        ],
        out_specs=pl.BlockSpec((TM, D), lambda i, c: (i, 0)),
        out_shape=jax.ShapeDtypeStruct((R, D), f32),
        scratch_shapes=[pltpu.VMEM((TM + 2 * HALO, D), bf16), pltpu.VMEM((TM + 2 * HALO, TC), f32)],
        compiler_params=_cparams(("parallel", "arbitrary")),
        name="gated_conv",
    )(h, h, h, g.reshape(1, D), w_in, w_in, w_in, conv_w, w_out)


NJ = 16


def _dft_factors(L):
    best = None
    for l1 in range(16, min(L, 128) + 1, 16):
        if L % l1:
            continue
        l2p = -(-(L // l1) // NJ) * NJ
        cost = 4 * l1 * l1 * NJ * l2p + 8 * l2p * 256 * -(-l2p // 256) * l1
        if best is None or cost < best[0]:
            best = (cost, l1, L // l1, l2p)
    return best[1:]


def _dft_tables(L, L1, L2, L2p):
    def cs(num, den):
        ang = (2.0 * math.pi / den) * (num % den).astype(f32)
        return jnp.cos(ang), jnp.sin(ang)

    i1 = jnp.arange(L1)
    c1, s1 = cs(i1[:, None] * i1[None, :], L1)
    eye = jnp.eye(NJ, dtype=f32)
    w1 = jnp.concatenate([jnp.kron(c1, eye), jnp.kron(-s1, eye)], axis=0).astype(bf16)
    n2 = jnp.arange(L2p)
    tc, ts = cs(i1[:, None] * n2[None, :], L)
    live = (n2 < L2)[None, :] * (L ** -0.5)

    def block_rows(t):
        return (t * live).reshape(L1, L2p // NJ, NJ).transpose(1, 0, 2).reshape(L2p // NJ, L1 * NJ, 1)

    tw = jnp.stack([block_rows(tc), block_rows(ts)], axis=1)
    c3, s3 = cs(n2[:, None] * n2[None, :], L2)
    inside = jnp.logical_and(n2[:, None] < L2, n2[None, :] < L2)
    w3 = jnp.stack([jnp.where(inside, c3, 0.0), jnp.where(inside, s3, 0.0)]).astype(bf16)
    ic = jnp.arange(FGROUP)
    cc, sc = cs(ic[:, None] * ic[None, :], FGROUP)
    wc = (jnp.stack([cc, sc]) * (FGROUP ** -0.5)).astype(bf16)
    return w1, tw, w3, wc


def _dft1_kernel(x_ref, w_ref, tw_ref, tr_ref, ti_ref):
    l1, nj, dc = x_ref.shape
    rows = l1 * nj
    t = _dot(w_ref[...], x_ref[...].reshape(rows, dc))
    tr, ti = t[:rows], t[rows:]
    tc, ts = tw_ref[0], tw_ref[1]
    tr_ref[...] = (tr * tc + ti * ts).astype(bf16).reshape(l1, nj, dc)
    ti_ref[...] = (ti * tc - tr * ts).astype(bf16).reshape(l1, nj, dc)


def _dft3_kernel(tr_ref, ti_ref, w3_ref, wc_ref, o_ref):
    c3, s3 = w3_ref[0], w3_ref[1]
    tr, ti = tr_ref[...], ti_ref[...]
    yr = (_dot(c3, tr) + _dot(s3, ti)).astype(bf16)
    yi = (_dot(c3, ti) - _dot(s3, tr)).astype(bf16)
    cc, sc = wc_ref[0], wc_ref[1]
    for gidx in range(o_ref.shape[1] // FGROUP):
        sl = slice(gidx * FGROUP, (gidx + 1) * FGROUP)
        o_ref[:, sl] = (_dot(yr[:, sl], cc) + _dot(yi[:, sl], sc)).astype(bf16)


def _fourier_mix(hn, B, Lp):
    L = Lp - META0
    L1, L2, L2p = _dft_factors(L)
    w1, tw, w3, wc = _dft_tables(L, L1, L2, L2p)
    x = hn.reshape(B, Lp, D)[:, META0:].reshape(B, L1, L2, D)
    x = jnp.pad(x, ((0, 0), (0, 0), (0, L2p - L2), (0, 0)))
    dc1 = 512
    rows = L1 * NJ
    t_shape = jax.ShapeDtypeStruct((B, L1, L2p, D), bf16)
    t_spec = pl.BlockSpec((None, L1, NJ, dc1), lambda b, n, c: (b, 0, n, c))
    tr, ti = pl.pallas_call(
        _dft1_kernel,
        grid=(B, L2p // NJ, D // dc1),
        in_specs=[
            t_spec,
            pl.BlockSpec((2 * rows, rows), lambda b, n, c: (0, 0)),
            pl.BlockSpec((None, 2, rows, 1), lambda b, n, c: (n, 0, 0, 0)),
        ],
        out_specs=[t_spec, t_spec],
        out_shape=[t_shape, t_shape],
        compiler_params=_cparams(("parallel", "parallel", "arbitrary")),
        name="seq_dft_stage1",
    )(x, w1, tw)
    dc3 = D if L2p <= 256 else 512
    nc3 = D // dc3
    slab = pl.BlockSpec((None, None, L2p, dc3), lambda b, k, c: (b, k, 0, c))
    mixed = pl.pallas_call(
        _dft3_kernel,
        grid=(B, L1, nc3),
        in_specs=[
            slab,
            slab,
            pl.BlockSpec((2, L2p, L2p), lambda b, k, c: (0, 0, 0)),
            pl.BlockSpec((2, FGROUP, FGROUP), lambda b, k, c: (0, 0, 0)),
        ],
        out_specs=pl.BlockSpec((None, L2p, dc3), lambda b, k, c: (b, 0, k * nc3 + c)),
        out_shape=jax.ShapeDtypeStruct((B, L2p, L1 * D), bf16),
        compiler_params=_cparams(("parallel", "parallel", "arbitrary")),
        name="seq_dft_stage2_channel_dft",
    )(tr, ti, w3, wc)
    mixed = mixed.reshape(B, L2p * L1, D)[:, :L]
    return jnp.pad(mixed, ((0, 0), (META0, 0), (0, 0))).reshape(B * Lp, D)


def _encode(x, meta, p):
    B, S, _ = x.shape
    Lp = S + FRONT
    front = jnp.concatenate([jnp.zeros((META0, D), f32), meta.astype(f32)], axis=0)
    h = jnp.concatenate([jnp.broadcast_to(front[None], (B, FRONT, D)), x], axis=1).reshape(B * Lp, D)

    qkv = _qkv(h, p["ln_mix"][0], *_rope_permute(p["a_wqkv"], p["a_q_norm"], p["a_k_norm"]),
               _rope_tables(S, B))
    h = _ffn(h, p["ln_ffn"][0], p["ffn_w_in"][0], p["ffn_w_out"][0], (_flash(qkv, B, Lp), p["a_wo"]))
    h = _ffn(h, p["ln_ffn"][1], p["ffn_w_in"][1], p["ffn_w_out"][1],
             (_fourier_mix(_norm(h, p["ln_mix"][1]), B, Lp), p["b_w"]))
    h = _conv_mixer(h, p["ln_mix"][2], p["c_w_in"], p["c_conv"], p["c_w_out"])
    h = _ffn(h, p["ln_ffn"][2], p["ffn_w_in"][2], p["ffn_w_out"][2])
    qkv = _qkv(h, p["ln_mix"][3], p["d_wqkv"], p["d_q_norm"], p["d_k_norm"], None)
    h = _ffn(h, p["ln_ffn"][3], p["ffn_w_in"][3], p["ffn_w_out"][3],
             (_win_attention(qkv, p["d_sink"], B, Lp), p["d_wo"]))
    return h.reshape(B, Lp, D)[:, FRONT:]


def kernel(x_prompt, x_sample, meta_tokens, ln_mix, ln_ffn, a_wqkv, a_q_norm, a_k_norm, a_wo, b_w,
           c_w_in, c_conv, c_w_out, d_wqkv, d_q_norm, d_k_norm, d_sink, d_wo, ffn_w_in, ffn_w_out):
    p = {
        "ln_mix": ln_mix, "ln_ffn": ln_ffn,
        "a_wqkv": a_wqkv[0].astype(bf16), "a_q_norm": a_q_norm[0], "a_k_norm": a_k_norm[0],
        "a_wo": a_wo[0].astype(bf16), "b_w": b_w[0].astype(bf16),
        "c_w_in": c_w_in[0].astype(bf16), "c_conv": c_conv[0], "c_w_out": c_w_out[0].astype(bf16),
        "d_wqkv": d_wqkv[0].astype(bf16), "d_q_norm": d_q_norm[0], "d_k_norm": d_k_norm[0],
        "d_sink": d_sink[0], "d_wo": d_wo[0].astype(bf16),
        "ffn_w_in": [ffn_w_in[i].astype(bf16) for i in range(ffn_w_in.shape[0])],
        "ffn_w_out": [ffn_w_out[i].astype(bf16) for i in range(ffn_w_out.shape[0])],
    }
    return _encode(x_prompt, meta_tokens, p), _encode(x_sample, meta_tokens, p)
```

```python
import functools
import math

import numpy as np
import jax
import jax.numpy as jnp
from jax import lax
from jax.experimental import pallas as pl
from jax.experimental.pallas import tpu as pltpu

D = 2048
N_HEADS = 16
N_KV = 4
GROUP = N_HEADS // N_KV
DH = D // N_HEADS
QW = GROUP * DH
KV_DIM = N_KV * DH
QKV_DIM = D + 2 * KV_DIM
WINDOW = 128
ROPE_THETA = 10000.0
GRID_W = 64
FGROUP = 256
D_FF = 5632
N_META = 16
EPS = 1e-6
NEG = -1e30
LOG2E = 1.4426950408889634
Q_SCALE = DH ** -0.5 * LOG2E

FRONT = 256
META0 = FRONT - N_META
QB = 256
TM = 512
HALO = 16
SKEW = 1
VMEM_LIMIT = 52 * 1024 * 1024

bf16 = jnp.bfloat16
f32 = jnp.float32


def _cparams(sem):
    return pltpu.CompilerParams(dimension_semantics=sem, vmem_limit_bytes=VMEM_LIMIT)


def _rms(x, g):
    return x * lax.rsqrt(jnp.mean(x * x, axis=-1, keepdims=True) + EPS) * g


def _dot(a, b):
    return jnp.dot(a, b, preferred_element_type=f32)


def _dot_nt(a, b):
    return lax.dot_general(a, b, (((1,), (1,)), ((), ())), preferred_element_type=f32)


def _h0_index(t, nblk, sblk):
    return (t // nblk) * sblk + jnp.maximum(t % nblk - 1, 0)


def _tile_rows(i, piece_refs, front_ref, nblk):
    pieces = [r[...] for r in piece_refs]
    if front_ref is not None:
        n = len(pieces)
        pieces = [jnp.where(lax.rem(n * i + k, nblk) == 0, front_ref[...], pc) for k, pc in enumerate(pieces)]
    return pieces[0] if len(pieces) == 1 else jnp.concatenate(pieces, axis=0)


def _norm_kernel(h_ref, g_ref, o_ref):
    o_ref[...] = _rms(h_ref[...], g_ref[...]).astype(bf16)


def _norm(h, g):
    R = h.shape[0]
    return pl.pallas_call(
        _norm_kernel,
        grid=(R // TM,),
        in_specs=[pl.BlockSpec((TM, D), lambda i: (i, 0)), pl.BlockSpec((1, D), lambda i: (0, 0))],
        out_specs=pl.BlockSpec((TM, D), lambda i: (i, 0)),
        out_shape=jax.ShapeDtypeStruct((R, D), bf16),
        compiler_params=_cparams(("parallel",)),
        name="rms_norm",
    )(h, g.reshape(1, D))


def _head_norm(yh, gain):
    return yh * lax.rsqrt(jnp.mean(yh * yh, axis=-1, keepdims=True) + EPS) * gain


def _qkv_kernel(*refs, rope, n_h, nblk):
    refs = list(refs)
    h_refs = [refs.pop(0) for _ in range(n_h)]
    front_ref = refs.pop(0) if nblk else None
    if rope:
        g_ref, w_ref, hg_ref, cos_ref, sin_ref, o_ref, xn_ref = refs
    else:
        g_ref, w_ref, hg_ref, o_ref, xn_ref = refs
    h = _tile_rows(pl.program_id(0), h_refs, front_ref, nblk)
    xn_ref[...] = _rms(h, g_ref[...]).astype(bf16)
    n_q = D // QW

    def project(j):
        return _dot(xn_ref[...], w_ref[:, j * QW:(j + 1) * QW])

    def finish(j, y):
        if j > n_q:
            o_ref[:, j * QW:(j + 1) * QW] = y.astype(bf16)
            return
        gain = hg_ref[0:1, :] if j < n_q else hg_ref[1:2, :]
        scale = Q_SCALE if j < n_q else 1.0
        for hh in range(GROUP):
            n = _head_norm(y[:, hh * DH:(hh + 1) * DH], gain)
            if rope:
                n = n * cos_ref[...] + pltpu.roll(n, DH // 2, 1) * sin_ref[...]
            o_ref[:, j * QW + hh * DH:j * QW + (hh + 1) * DH] = (n * scale).astype(bf16)

    n_tiles = QKV_DIM // QW
    y = project(0)
    for j in range(n_tiles):
        y_next = project(j + 1) if j + 1 < n_tiles else None
        finish(j, y)
        y = y_next


def _qkv(h, g, w, q_gain, k_gain, rope_tabs, h0=None):
    rope = rope_tabs is not None
    if h0 is None:
        R, nblk = h.shape[0], 0
        in_specs, args = [pl.BlockSpec((TM, D), lambda i: (i, 0))], [h]
    else:
        B, S, front = h0
        R, nblk, sblk, npc = B * (S + FRONT), (S + FRONT) // FRONT, S // FRONT, TM // FRONT
        in_specs = [pl.BlockSpec((FRONT, D), functools.partial(
            lambda i, k: (_h0_index(npc * i + k, nblk, sblk), 0), k=k)) for k in range(npc)]
        in_specs.append(pl.BlockSpec((FRONT, D), lambda i: (0, 0)))
        args = [h] * npc + [front]
    n_h = len(args) - (nblk > 0)
    in_specs += [
        pl.BlockSpec((1, D), lambda i: (0, 0)),
        pl.BlockSpec((D, QKV_DIM), lambda i: (0, 0)),
        pl.BlockSpec((2, DH), lambda i: (0, 0)),
    ]
    args += [g.reshape(1, D), w, jnp.stack([q_gain, k_gain])]
    if rope:
        in_specs += [pl.BlockSpec((TM, DH), lambda i: (i, 0))] * 2
        args += list(rope_tabs)
    return pl.pallas_call(
        functools.partial(_qkv_kernel, rope=rope, n_h=n_h, nblk=nblk),
        grid=(R // TM,),
        in_specs=in_specs,
        out_specs=pl.BlockSpec((TM, QKV_DIM), lambda i: (i, 0)),
        out_shape=jax.ShapeDtypeStruct((R, QKV_DIM), bf16),
        scratch_shapes=[pltpu.VMEM((TM, D), bf16)],
        compiler_params=_cparams(("parallel",)),
        name="qkv_proj",
    )(*args)


ROPE_PERM = np.concatenate([np.arange(0, 32), np.arange(64, 96), np.arange(32, 64), np.arange(96, 128)])


def _rope_permute(wqkv, q_gain, k_gain):
    n_qk = N_HEADS + N_KV
    cols = (np.arange(n_qk)[:, None] * DH + ROPE_PERM[None, :]).reshape(-1)
    cols = np.concatenate([cols, np.arange(n_qk * DH, QKV_DIM)])
    return wqkv[:, cols], q_gain[ROPE_PERM], k_gain[ROPE_PERM]


def _rope_tables(S, B):
    quarter = DH // 4
    inv = ROPE_THETA ** (-jnp.arange(quarter, dtype=f32) / quarter)
    t = jnp.arange(S)
    row = jnp.concatenate([jnp.zeros((FRONT,), f32), (t // GRID_W).astype(f32)])
    col = jnp.concatenate([jnp.zeros((FRONT,), f32), (t % GRID_W).astype(f32)])
    ang_r = row[:, None] * inv[None, :]
    ang_c = col[:, None] * inv[None, :]
    ang = jnp.concatenate([ang_r, ang_c, ang_r, ang_c], axis=1)
    cos, sin = jnp.cos(ang), jnp.sin(ang)
    sin = jnp.where((np.arange(DH) < DH // 2)[None, :], -sin, sin)
    return jnp.tile(cos, (B, 1)), jnp.tile(sin, (B, 1))


def _flash_kernel(q_ref, k_ref, v_ref, o_ref, m_ref, l_ref, acc_ref, *, n_chunks, tk):
    qi = pl.program_id(2)
    col0 = lax.broadcasted_iota(jnp.int32, (QB, DH), 1) + (FRONT - DH)

    def with_ones(v):
        return jnp.concatenate([v, jnp.ones(v.shape, bf16)], axis=1)

    k0 = k_ref[FRONT - DH:FRONT, :]
    v0 = with_ones(v_ref[FRONT - DH:FRONT, :])
    for hh in range(GROUP):
        s = jnp.where(col0 >= META0, _dot_nt(q_ref[:, hh * DH:(hh + 1) * DH], k0), NEG)
        m = jnp.max(s, axis=-1, keepdims=True)
        pv = _dot(jnp.exp2(s - m).astype(bf16), v0)
        m_ref[hh] = jnp.broadcast_to(m, (QB, DH))
        l_ref[hh] = pv[:, DH:]
        acc_ref[hh] = pv[:, :DH]

    def scores(hh, kc):
        return _dot_nt(q_ref[:, hh * DH:(hh + 1) * DH], kc)

    def update(hh, s, vc):
        m_prev = m_ref[hh]
        m_new = jnp.maximum(m_prev, jnp.max(s, axis=-1, keepdims=True))
        p = jnp.exp2((s - jnp.concatenate([m_new] * (tk // DH), axis=1)).astype(bf16))
        alpha = jnp.exp2(m_prev - m_new)
        pv = _dot(p, vc)
        l_ref[hh] = alpha * l_ref[hh] + pv[:, DH:]
        acc_ref[hh] = alpha * acc_ref[hh] + pv[:, :DH]
        m_ref[hh] = m_new

    def chunk(kc, vc, pending, k_next):
        pending = list(pending)
        for hh in range(GROUP):
            ahead = hh + SKEW
            if ahead < GROUP:
                pending.append(scores(ahead, kc))
            elif k_next is not None:
                pending.append(scores(ahead - GROUP, k_next))
            update(hh, pending.pop(0), vc)
        return tuple(pending)

    def body(c, pending):
        start = pl.multiple_of(FRONT + c * tk, FRONT)
        return chunk(k_ref[pl.ds(start, tk), :], with_ones(v_ref[pl.ds(start, tk), :]), pending,
                     k_ref[pl.ds(start + tk, tk), :])

    k_first = k_ref[pl.ds(FRONT, tk), :]
    pending = lax.fori_loop(0, n_chunks - 1, body, tuple(scores(hh, k_first) for hh in range(SKEW)),
                            unroll=4)
    last = FRONT + (n_chunks - 1) * tk
    chunk(k_ref[pl.ds(last, tk), :], with_ones(v_ref[pl.ds(last, tk), :]), pending, None)
    rowi = lax.broadcasted_iota(jnp.int32, (QB, DH), 0)
    keep_rows = jnp.logical_or(qi > 0, rowi >= META0)
    for hh in range(GROUP):
        o_ref[:, hh * DH:(hh + 1) * DH] = jnp.where(keep_rows, acc_ref[hh] / l_ref[hh], 0.0).astype(bf16)


def _flash(qkv, B, Lp):
    R = qkv.shape[0]
    S = Lp - FRONT
    tk = 512 if S % 512 == 0 else 256
    nq = Lp // QB
    kcol, vcol = D // DH, (D + KV_DIM) // DH
    return pl.pallas_call(
        functools.partial(_flash_kernel, n_chunks=S // tk, tk=tk),
        grid=(B, N_KV, nq),
        in_specs=[
            pl.BlockSpec((QB, QW), lambda b, g, i: (b * nq + i, g)),
            pl.BlockSpec((Lp, DH), lambda b, g, i: (b, kcol + g)),
            pl.BlockSpec((Lp, DH), lambda b, g, i: (b, vcol + g)),
        ],
        out_specs=pl.BlockSpec((QB, QW), lambda b, g, i: (b * nq + i, g)),
        out_shape=jax.ShapeDtypeStruct((R, D), bf16),
        scratch_shapes=[pltpu.VMEM((GROUP, QB, DH), f32)] * 3,
        compiler_params=_cparams(("parallel", "parallel", "arbitrary")),
        name="dense_attention",
    )(qkv, qkv, qkv)


def _win_kernel(hs_ref, q_ref, k_ref, v_ref, o_ref, *, Lp):
    g = pl.program_id(1)
    qi = pl.program_id(2)
    band = 2 * QB
    start = pl.multiple_of(jnp.clip(QB * qi - WINDOW, 0, Lp - band), WINDOW)
    kb = k_ref[pl.ds(start, band), :]
    vb = v_ref[pl.ds(start, band), :]
    km = k_ref[FRONT - DH:FRONT, :]
    vm = v_ref[FRONT - DH:FRONT, :]
    meta_ok = lax.broadcasted_iota(jnp.int32, (QB, DH), 1) >= DH - N_META
    qpos = QB * qi + lax.broadcasted_iota(jnp.int32, (QB, band), 0)
    kpos = start + lax.broadcasted_iota(jnp.int32, (QB, band), 1)
    dist = jnp.abs(qpos - kpos)
    ok = jnp.logical_and(dist <= WINDOW, kpos >= FRONT)
    distf = dist.astype(f32)
    rowi = lax.broadcasted_iota(jnp.int32, (QB, DH), 0)
    keep_rows = jnp.logical_or(qi > 0, rowi >= META0)

    def scores(hh):
        q = q_ref[:, hh * DH:(hh + 1) * DH]
        return _dot_nt(q, kb), _dot_nt(q, km)

    nxt = scores(0)
    for hh in range(GROUP):
        raw_b, raw_m = nxt
        if hh + 1 < GROUP:
            nxt = scores(hh + 1)
        slope = jnp.where(qi == 0, 0.0, hs_ref[0, g * GROUP + hh])
        sink = hs_ref[1, g * GROUP + hh]
        sb = jnp.where(ok, raw_b - slope * distf, NEG)
        sm = jnp.where(meta_ok, raw_m, NEG)
        m = jnp.maximum(jnp.max(sb, axis=-1, keepdims=True), jnp.max(sm, axis=-1, keepdims=True))
        m = jnp.maximum(m, sink)
        eb = jnp.exp2(sb - m)
        em = jnp.exp2(sm - m)
        den = (jnp.sum(eb, axis=-1, keepdims=True) + jnp.sum(em, axis=-1, keepdims=True)
               + jnp.exp2(sink - m))
        acc = _dot(eb.astype(bf16), vb) + _dot(em.astype(bf16), vm)
        o_ref[:, hh * DH:(hh + 1) * DH] = jnp.where(keep_rows, acc / den, 0.0).astype(bf16)


def _win_attention(qkv, sink, B, Lp):
    R = qkv.shape[0]
    nq = Lp // QB
    kcol, vcol = D // DH, (D + KV_DIM) // DH
    slopes = 2.0 ** (-8.0 * jnp.arange(1, N_HEADS + 1, dtype=f32) / N_HEADS)
    hs = jnp.stack([slopes, sink.astype(f32)]) * LOG2E
    return pl.pallas_call(
        functools.partial(_win_kernel, Lp=Lp),
        grid=(B, N_KV, nq),
        in_specs=[
            pl.BlockSpec(memory_space=pltpu.SMEM),
            pl.BlockSpec((QB, QW), lambda b, g, i: (b * nq + i, g)),
            pl.BlockSpec((Lp, DH), lambda b, g, i: (b, kcol + g)),
            pl.BlockSpec((Lp, DH), lambda b, g, i: (b, vcol + g)),
        ],
        out_specs=pl.BlockSpec((QB, QW), lambda b, g, i: (b * nq + i, g)),
        out_shape=jax.ShapeDtypeStruct((R, D), bf16),
        compiler_params=_cparams(("parallel", "parallel", "arbitrary")),
        name="window_attention",
    )(hs, qkv, qkv, qkv)


TF = 512


def _ffn_kernel(*refs, n_h, n_a, nblk, f_axis):
    refs = list(refs)
    h_refs = [refs.pop(0) for _ in range(n_h)]
    front_ref = refs.pop(0) if nblk else None
    a_refs = [refs.pop(0) for _ in range(n_a)]
    wp_ref = refs.pop(0) if n_a else None
    g_ref, wg_ref, wu_ref, wo_ref, o_ref, xn_ref = refs

    @pl.when(pl.program_id(f_axis) == 0)
    def _():
        x = _tile_rows(pl.program_id(0), h_refs, front_ref, nblk)
        if n_a:
            x = x + _dot(_tile_rows(0, a_refs, None, 0), wp_ref[...])
        xn_ref[...] = _rms(x, g_ref[...]).astype(bf16)
        o_ref[...] = x

    xn = xn_ref[...]
    gate = _dot(xn, wg_ref[...])
    up = _dot(xn, wu_ref[...])
    a = gate * jax.nn.sigmoid(gate) * up
    o_ref[...] += _dot(a.astype(bf16), wo_ref[...])


def _ffn(h, g, w_in, w_out, mix=None, h0=None, real=None):
    nf = D_FF // TF
    npc = TM // FRONT
    nblk = sblk = 0
    if h0 is not None or real is not None:
        B, S = (h0 or real)[:2]
        nblk, sblk = (S + FRONT) // FRONT, S // FRONT
    if real is None:
        R = h.shape[0] if h0 is None else B * (S + FRONT)
        grid = (R // TM, nf)
        out_spec = pl.BlockSpec((TM, D), lambda i, f: (i, 0))
        out_shape = jax.ShapeDtypeStruct((R, D), f32)
    else:
        grid = (B, S // TM, nf)
        out_spec = pl.BlockSpec((None, TM, D), lambda b, j, f: (b, j, 0))
        out_shape = jax.ShapeDtypeStruct((B, S, D), f32)

    def const(*block):
        return lambda *idx: block

    def piece(k):
        if real is None:
            return lambda i, f: (npc * i + k, 0)
        return lambda b, j, f: (b * nblk + 1 + npc * j + k, 0)

    if h0 is not None:
        in_specs = [pl.BlockSpec((FRONT, D), functools.partial(
            lambda i, f, k: (_h0_index(npc * i + k, nblk, sblk), 0), k=k)) for k in range(npc)]
        in_specs.append(pl.BlockSpec((FRONT, D), const(0, 0)))
        args = [h] * npc + [h0[2]]
    else:
        in_specs = [pl.BlockSpec((FRONT, D), piece(k)) for k in range(npc)]
        args = [h] * npc
    n_a = 0
    if mix is not None:
        n_a = npc
        in_specs += [pl.BlockSpec((FRONT, D), piece(k)) for k in range(npc)]
        in_specs.append(pl.BlockSpec((D, D), const(0, 0), pipeline_mode=pl.Buffered(1)))
        args += [mix[0]] * npc + [mix[1]]
    in_specs += [
        pl.BlockSpec((1, D), const(0, 0)),
        pl.BlockSpec((D, TF), lambda *idx: (0, idx[-1])),
        pl.BlockSpec((D, TF), lambda *idx: (0, nf + idx[-1])),
        pl.BlockSpec((TF, D), lambda *idx: (idx[-1], 0)),
    ]
    args += [g.reshape(1, D), w_in, w_in, w_out]
    return pl.pallas_call(
        functools.partial(_ffn_kernel, n_h=npc, n_a=n_a, nblk=nblk if h0 is not None else 0,
                          f_axis=len(grid) - 1),
        grid=grid,
        in_specs=in_specs,
        out_specs=out_spec,
        out_shape=out_shape,
        scratch_shapes=[pltpu.VMEM((TM, D), bf16)],
        compiler_params=_cparams(("parallel",) * (len(grid) - 1) + ("arbitrary",)),
        name="swiglu_ffn",
    )(*args)


TC = 512


def _conv_kernel(h_ref, hp_ref, hn_ref, g_ref, wb_ref, wc_ref, wu_ref, cw_ref, wo_ref, o_ref,
                 xn_ref, gu_ref):
    i = pl.program_id(0)

    @pl.when(pl.program_id(1) == 0)
    def _():
        x = h_ref[...]
        gain = g_ref[...]
        xn_ref[HALO:HALO + TM, :] = _rms(x, gain).astype(bf16)
        xn_ref[0:HALO, :] = _rms(hp_ref[...], gain).astype(bf16)
        nxt = _rms(hn_ref[...], gain)
        nxt = jnp.where(i == pl.num_programs(0) - 1, 0.0, nxt)
        xn_ref[HALO + TM:, :] = nxt.astype(bf16)
        o_ref[...] = x

    xn = xn_ref[...]
    gu_ref[...] = _dot(xn, wc_ref[...]) * _dot(xn, wu_ref[...])
    cw = cw_ref[...]
    conv = (gu_ref[HALO - 1:HALO - 1 + TM, :] * cw[0:1, :]
            + gu_ref[HALO:HALO + TM, :] * cw[1:2, :]
            + gu_ref[HALO + 1:HALO + 1 + TM, :] * cw[2:3, :])
    bgate = _dot(xn_ref[HALO:HALO + TM, :], wb_ref[...])
    o_ref[...] += _dot((bgate * conv).astype(bf16), wo_ref[...])


def _conv_mixer(h, g, w_in, conv_w, w_out):
    R = h.shape[0]
    nc = D // TC
    hb = TM // HALO
    last = R // HALO - 1
    return pl.pallas_call(
        _conv_kernel,
        grid=(R // TM, nc),
        in_specs=[
            pl.BlockSpec((TM, D), lambda i, c: (i, 0)),
            pl.BlockSpec((HALO, D), lambda i, c: (jnp.maximum(i * hb - 1, 0), 0)),
            pl.BlockSpec((HALO, D), lambda i, c: (jnp.minimum((i + 1) * hb, last), 0)),
            pl.BlockSpec((1, D), lambda i, c: (0, 0)),
            pl.BlockSpec((D, TC), lambda i, c: (0, c)),
            pl.BlockSpec((D, TC), lambda i, c: (0, nc + c)),
            pl.BlockSpec((D, TC), lambda i, c: (0, 2 * nc + c)),
            pl.BlockSpec((3, TC), lambda i, c: (0, c)),
            pl.BlockSpec((TC, D), lambda i, c: (c, 0)),
        ],
        out_specs=pl.BlockSpec((TM, D), lambda i, c: (i, 0)),
        out_shape=jax.ShapeDtypeStruct((R, D), f32),
        scratch_shapes=[pltpu.VMEM((TM + 2 * HALO, D), bf16), pltpu.VMEM((TM + 2 * HALO, TC), f32)],
        compiler_params=_cparams(("parallel", "arbitrary")),
        name="gated_conv",
    )(h, h, h, g.reshape(1, D), w_in, w_in, w_in, conv_w, w_out)


NJ = 16


def _dft_factors(L):
    best = None
    for l1 in range(16, min(L, 128) + 1, 16):
        if L % l1:
            continue
        l2p = -(-(L // l1) // NJ) * NJ
        cost = 4 * l1 * l1 * NJ * l2p + 8 * l2p * 256 * -(-l2p // 256) * l1
        if best is None or cost < best[0]:
            best = (cost, l1, L // l1, l2p)
    return best[1:]


def _dft_tables(L, L1, L2, L2p):
    def cs(num, den):
        ang = (2.0 * math.pi / den) * (num % den).astype(f32)
        return jnp.cos(ang), jnp.sin(ang)

    i1 = jnp.arange(L1)
    c1, s1 = cs(i1[:, None] * i1[None, :], L1)
    eye = jnp.eye(NJ, dtype=f32)
    w1 = jnp.concatenate([jnp.kron(c1, eye), jnp.kron(-s1, eye)], axis=0).astype(bf16)
    n2 = jnp.arange(L2p)
    tc, ts = cs(i1[:, None] * n2[None, :], L)
    live = (n2 < L2)[None, :] * (L ** -0.5)

    def block_rows(t):
        return (t * live).reshape(L1, L2p // NJ, NJ).transpose(1, 0, 2).reshape(L2p // NJ, L1 * NJ, 1)

    tw = jnp.stack([block_rows(tc), block_rows(ts)], axis=1)
    c3, s3 = cs(n2[:, None] * n2[None, :], L2)
    inside = jnp.logical_and(n2[:, None] < L2, n2[None, :] < L2)
    w3 = jnp.stack([jnp.where(inside, c3, 0.0), jnp.where(inside, s3, 0.0)]).astype(bf16)
    ic = jnp.arange(FGROUP)
    cc, sc = cs(ic[:, None] * ic[None, :], FGROUP)
    wc = (jnp.stack([cc, sc]) * (FGROUP ** -0.5)).astype(bf16)
    return w1, tw, w3, wc


def _dft1_kernel(x_ref, w_ref, tw_ref, tr_ref, ti_ref):
    l1, nj, dc = x_ref.shape
    rows = l1 * nj
    t = _dot(w_ref[...], x_ref[...].reshape(rows, dc))
    tr, ti = t[:rows], t[rows:]
    tc, ts = tw_ref[0], tw_ref[1]
    tr_ref[...] = (tr * tc + ti * ts).astype(bf16).reshape(l1, nj, dc)
    ti_ref[...] = (ti * tc - tr * ts).astype(bf16).reshape(l1, nj, dc)


def _dft3_kernel(tr_ref, ti_ref, w3_ref, wc_ref, o_ref):
    c3, s3 = w3_ref[0], w3_ref[1]
    tr, ti = tr_ref[...], ti_ref[...]
    yr = (_dot(c3, tr) + _dot(s3, ti)).astype(bf16)
    yi = (_dot(c3, ti) - _dot(s3, tr)).astype(bf16)
    cc, sc = wc_ref[0], wc_ref[1]
    for gidx in range(o_ref.shape[1] // FGROUP):
        sl = slice(gidx * FGROUP, (gidx + 1) * FGROUP)
        o_ref[:, sl] = (_dot(yr[:, sl], cc) + _dot(yi[:, sl], sc)).astype(bf16)


def _fourier_mix(hn, B, Lp):
    L = Lp - META0
    L1, L2, L2p = _dft_factors(L)
    w1, tw, w3, wc = _dft_tables(L, L1, L2, L2p)
    x = hn.reshape(B, Lp, D)[:, META0:].reshape(B, L1, L2, D)
    x = jnp.pad(x, ((0, 0), (0, 0), (0, L2p - L2), (0, 0)))
    dc1 = 512
    rows = L1 * NJ
    t_shape = jax.ShapeDtypeStruct((B, L1, L2p, D), bf16)
    t_spec = pl.BlockSpec((None, L1, NJ, dc1), lambda b, n, c: (b, 0, n, c))
    tr, ti = pl.pallas_call(
        _dft1_kernel,
        grid=(B, L2p // NJ, D // dc1),
        in_specs=[
            t_spec,
            pl.BlockSpec((2 * rows, rows), lambda b, n, c: (0, 0)),
            pl.BlockSpec((None, 2, rows, 1), lambda b, n, c: (n, 0, 0, 0)),
        ],
        out_specs=[t_spec, t_spec],
        out_shape=[t_shape, t_shape],
        compiler_params=_cparams(("parallel", "parallel", "arbitrary")),
        name="seq_dft_stage1",
    )(x, w1, tw)
    dc3 = D if L2p <= 256 else 512
    nc3 = D // dc3
    slab = pl.BlockSpec((None, None, L2p, dc3), lambda b, k, c: (b, k, 0, c))
    mixed = pl.pallas_call(
        _dft3_kernel,
        grid=(B, L1, nc3),
        in_specs=[
            slab,
            slab,
            pl.BlockSpec((2, L2p, L2p), lambda b, k, c: (0, 0, 0)),
            pl.BlockSpec((2, FGROUP, FGROUP), lambda b, k, c: (0, 0, 0)),
        ],
        out_specs=pl.BlockSpec((None, L2p, dc3), lambda b, k, c: (b, 0, k * nc3 + c)),
        out_shape=jax.ShapeDtypeStruct((B, L2p, L1 * D), bf16),
        compiler_params=_cparams(("parallel", "parallel", "arbitrary")),
        name="seq_dft_stage2_channel_dft",
    )(tr, ti, w3, wc)
    mixed = mixed.reshape(B, L2p * L1, D)[:, :L]
    return jnp.pad(mixed, ((0, 0), (META0, 0), (0, 0))).reshape(B * Lp, D)


def _encode(x, meta, p):
    B, S, _ = x.shape
    Lp = S + FRONT
    front = jnp.concatenate([jnp.zeros((META0, D), f32), meta.astype(f32)], axis=0)
    h0 = (B, S, front)
    x2d = x.reshape(B * S, D)

    qkv = _qkv(x2d, p["ln_mix"][0], *_rope_permute(p["a_wqkv"], p["a_q_norm"], p["a_k_norm"]),
               _rope_tables(S, B), h0=h0)
    h = _ffn(x2d, p["ln_ffn"][0], p["ffn_w_in"][0], p["ffn_w_out"][0], (_flash(qkv, B, Lp), p["a_wo"]),
             h0=h0)
    h = _ffn(h, p["ln_ffn"][1], p["ffn_w_in"][1], p["ffn_w_out"][1],
             (_fourier_mix(_norm(h, p["ln_mix"][1]), B, Lp), p["b_w"]))
    h = _conv_mixer(h, p["ln_mix"][2], p["c_w_in"], p["c_conv"], p["c_w_out"])
    h = _ffn(h, p["ln_ffn"][2], p["ffn_w_in"][2], p["ffn_w_out"][2])
    qkv = _qkv(h, p["ln_mix"][3], p["d_wqkv"], p["d_q_norm"], p["d_k_norm"], None)
    return _ffn(h, p["ln_ffn"][3], p["ffn_w_in"][3], p["ffn_w_out"][3],
                (_win_attention(qkv, p["d_sink"], B, Lp), p["d_wo"]), real=(B, S))


def kernel(x_prompt, x_sample, meta_tokens, ln_mix, ln_ffn, a_wqkv, a_q_norm, a_k_norm, a_wo, b_w,
           c_w_in, c_conv, c_w_out, d_wqkv, d_q_norm, d_k_norm, d_sink, d_wo, ffn_w_in, ffn_w_out):
    p = {
        "ln_mix": ln_mix, "ln_ffn": ln_ffn,
        "a_wqkv": a_wqkv[0].astype(bf16), "a_q_norm": a_q_norm[0], "a_k_norm": a_k_norm[0],
        "a_wo": a_wo[0].astype(bf16), "b_w": b_w[0].astype(bf16),
        "c_w_in": c_w_in[0].astype(bf16), "c_conv": c_conv[0], "c_w_out": c_w_out[0].astype(bf16),
        "d_wqkv": d_wqkv[0].astype(bf16), "d_q_norm": d_q_norm[0], "d_k_norm": d_k_norm[0],
        "d_sink": d_sink[0], "d_wo": d_wo[0].astype(bf16),
        "ffn_w_in": [ffn_w_in[i].astype(bf16) for i in range(ffn_w_in.shape[0])],
        "ffn_w_out": [ffn_w_out[i].astype(bf16) for i in range(ffn_w_out.shape[0])],
    }
    return _encode(x_prompt, meta_tokens, p), _encode(x_sample, meta_tokens, p)
```

```python
import functools
import math

import numpy as np
import jax
import jax.numpy as jnp
from jax import lax
from jax.experimental import pallas as pl
from jax.experimental.pallas import tpu as pltpu

D = 2048
N_HEADS = 16
N_KV = 4
GROUP = N_HEADS // N_KV
DH = D // N_HEADS
QW = GROUP * DH
KV_DIM = N_KV * DH
QKV_DIM = D + 2 * KV_DIM
WINDOW = 128
ROPE_THETA = 10000.0
GRID_W = 64
FGROUP = 256
D_FF = 5632
N_META = 16
EPS = 1e-6
NEG = -1e30
LOG2E = 1.4426950408889634
Q_SCALE = DH ** -0.5 * LOG2E

FRONT = 256
META0 = FRONT - N_META
QB = 256
TM = 512
HALO = 16
SKEW = 1
VMEM_LIMIT = 52 * 1024 * 1024

bf16 = jnp.bfloat16
f32 = jnp.float32


def _cparams(sem):
    return pltpu.CompilerParams(dimension_semantics=sem, vmem_limit_bytes=VMEM_LIMIT)


def _rms(x, g):
    return x * lax.rsqrt(jnp.mean(x * x, axis=-1, keepdims=True) + EPS) * g


def _dot(a, b):
    return jnp.dot(a, b, preferred_element_type=f32)


def _dot_nt(a, b):
    return lax.dot_general(a, b, (((1,), (1,)), ((), ())), preferred_element_type=f32)


def _h0_index(t, nblk, sblk):
    return (t // nblk) * sblk + jnp.maximum(t % nblk - 1, 0)


def _tile_rows(i, piece_refs, front_ref, nblk):
    pieces = [r[...] for r in piece_refs]
    if front_ref is not None:
        n = len(pieces)
        pieces = [jnp.where(lax.rem(n * i + k, nblk) == 0, front_ref[...], pc) for k, pc in enumerate(pieces)]
    return pieces[0] if len(pieces) == 1 else jnp.concatenate(pieces, axis=0)


def _norm_kernel(h_ref, g_ref, o_ref):
    o_ref[...] = _rms(h_ref[...], g_ref[...]).astype(bf16)


def _norm(h, g):
    R = h.shape[0]
    return pl.pallas_call(
        _norm_kernel,
        grid=(R // TM,),
        in_specs=[pl.BlockSpec((TM, D), lambda i: (i, 0)), pl.BlockSpec((1, D), lambda i: (0, 0))],
        out_specs=pl.BlockSpec((TM, D), lambda i: (i, 0)),
        out_shape=jax.ShapeDtypeStruct((R, D), bf16),
        compiler_params=_cparams(("parallel",)),
        name="rms_norm",
    )(h, g.reshape(1, D))


def _head_norm(yh, gain):
    return yh * lax.rsqrt(jnp.mean(yh * yh, axis=-1, keepdims=True) + EPS) * gain


def _qkv_kernel(*refs, rope, n_h, nblk):
    refs = list(refs)
    h_refs = [refs.pop(0) for _ in range(n_h)]
    front_ref = refs.pop(0) if nblk else None
    if rope:
        g_ref, w_ref, hg_ref, cos_ref, sin_ref, o_ref, xn_ref = refs
    else:
        g_ref, w_ref, hg_ref, o_ref, xn_ref = refs
    h = _tile_rows(pl.program_id(0), h_refs, front_ref, nblk)
    xn_ref[...] = _rms(h, g_ref[...]).astype(bf16)
    n_q = D // QW

    def project(j):
        return _dot(xn_ref[...], w_ref[:, j * QW:(j + 1) * QW])

    def finish(j, y):
        if j > n_q:
            o_ref[:, j * QW:(j + 1) * QW] = y.astype(bf16)
            return
        gain = hg_ref[0:1, :] if j < n_q else hg_ref[1:2, :]
        scale = Q_SCALE if j < n_q else 1.0
        for hh in range(GROUP):
            n = _head_norm(y[:, hh * DH:(hh + 1) * DH], gain)
            if rope:
                n = n * cos_ref[...] + pltpu.roll(n, DH // 2, 1) * sin_ref[...]
            o_ref[:, j * QW + hh * DH:j * QW + (hh + 1) * DH] = (n * scale).astype(bf16)

    n_tiles = QKV_DIM // QW
    y = project(0)
    for j in range(n_tiles):
        y_next = project(j + 1) if j + 1 < n_tiles else None
        finish(j, y)
        y = y_next


def _qkv(h, g, w, q_gain, k_gain, rope_tabs, h0=None):
    rope = rope_tabs is not None
    if h0 is None:
        R, nblk = h.shape[0], 0
        in_specs, args = [pl.BlockSpec((TM, D), lambda i: (i, 0))], [h]
    else:
        B, S, front = h0
        R, nblk, sblk, npc = B * (S + FRONT), (S + FRONT) // FRONT, S // FRONT, TM // FRONT
        in_specs = [pl.BlockSpec((FRONT, D), functools.partial(
            lambda i, k: (_h0_index(npc * i + k, nblk, sblk), 0), k=k)) for k in range(npc)]
        in_specs.append(pl.BlockSpec((FRONT, D), lambda i: (0, 0)))
        args = [h] * npc + [front]
    n_h = len(args) - (nblk > 0)
    in_specs += [
        pl.BlockSpec((1, D), lambda i: (0, 0)),
        pl.BlockSpec((D, QKV_DIM), lambda i: (0, 0)),
        pl.BlockSpec((2, DH), lambda i: (0, 0)),
    ]
    args += [g.reshape(1, D), w, jnp.stack([q_gain, k_gain])]
    if rope:
        in_specs += [pl.BlockSpec((TM, DH), lambda i: (i, 0))] * 2
        args += list(rope_tabs)
    return pl.pallas_call(
        functools.partial(_qkv_kernel, rope=rope, n_h=n_h, nblk=nblk),
        grid=(R // TM,),
        in_specs=in_specs,
        out_specs=pl.BlockSpec((TM, QKV_DIM), lambda i: (i, 0)),
        out_shape=jax.ShapeDtypeStruct((R, QKV_DIM), bf16),
        scratch_shapes=[pltpu.VMEM((TM, D), bf16)],
        compiler_params=_cparams(("parallel",)),
        name="qkv_proj",
    )(*args)


ROPE_PERM = np.concatenate([np.arange(0, 32), np.arange(64, 96), np.arange(32, 64), np.arange(96, 128)])


def _rope_permute(wqkv, q_gain, k_gain):
    n_qk = N_HEADS + N_KV
    cols = (np.arange(n_qk)[:, None] * DH + ROPE_PERM[None, :]).reshape(-1)
    cols = np.concatenate([cols, np.arange(n_qk * DH, QKV_DIM)])
    return wqkv[:, cols], q_gain[ROPE_PERM], k_gain[ROPE_PERM]


def _rope_tables(S, B):
    quarter = DH // 4
    inv = ROPE_THETA ** (-jnp.arange(quarter, dtype=f32) / quarter)
    t = jnp.arange(S)
    row = jnp.concatenate([jnp.zeros((FRONT,), f32), (t // GRID_W).astype(f32)])
    col = jnp.concatenate([jnp.zeros((FRONT,), f32), (t % GRID_W).astype(f32)])
    ang_r = row[:, None] * inv[None, :]
    ang_c = col[:, None] * inv[None, :]
    ang = jnp.concatenate([ang_r, ang_c, ang_r, ang_c], axis=1)
    cos, sin = jnp.cos(ang), jnp.sin(ang)
    sin = jnp.where((np.arange(DH) < DH // 2)[None, :], -sin, sin)
    return jnp.tile(cos, (B, 1)), jnp.tile(sin, (B, 1))


def _flash_kernel(q_ref, k_ref, v_ref, o_ref, m_ref, l_ref, acc_ref, *, n_chunks, tk):
    qi = pl.program_id(2)
    col0 = lax.broadcasted_iota(jnp.int32, (QB, DH), 1) + (FRONT - DH)

    def with_ones(v):
        return jnp.concatenate([v, jnp.ones(v.shape, bf16)], axis=1)

    k0 = k_ref[FRONT - DH:FRONT, :]
    v0 = with_ones(v_ref[FRONT - DH:FRONT, :])
    for hh in range(GROUP):
        s = jnp.where(col0 >= META0, _dot_nt(q_ref[:, hh * DH:(hh + 1) * DH], k0), NEG)
        m = jnp.max(s, axis=-1, keepdims=True)
        pv = _dot(jnp.exp2(s - m).astype(bf16), v0)
        m_ref[hh] = jnp.broadcast_to(m, (QB, DH))
        l_ref[hh] = pv[:, DH:]
        acc_ref[hh] = pv[:, :DH]

    def scores(hh, kc):
        return _dot_nt(q_ref[:, hh * DH:(hh + 1) * DH], kc)

    def update(hh, s, vc):
        m_prev = m_ref[hh]
        m_new = jnp.maximum(m_prev, jnp.max(s, axis=-1, keepdims=True))
        p = jnp.exp2((s - jnp.concatenate([m_new] * (tk // DH), axis=1)).astype(bf16))
        alpha = jnp.exp2(m_prev - m_new)
        pv = _dot(p, vc)
        l_ref[hh] = alpha * l_ref[hh] + pv[:, DH:]
        acc_ref[hh] = alpha * acc_ref[hh] + pv[:, :DH]
        m_ref[hh] = m_new

    def chunk(kc, vc, pending, k_next):
        pending = list(pending)
        for hh in range(GROUP):
            ahead = hh + SKEW
            if ahead < GROUP:
                pending.append(scores(ahead, kc))
            elif k_next is not None:
                pending.append(scores(ahead - GROUP, k_next))
            update(hh, pending.pop(0), vc)
        return tuple(pending)

    def body(c, pending):
        start = pl.multiple_of(FRONT + c * tk, FRONT)
        return chunk(k_ref[pl.ds(start, tk), :], with_ones(v_ref[pl.ds(start, tk), :]), pending,
                     k_ref[pl.ds(start + tk, tk), :])

    k_first = k_ref[pl.ds(FRONT, tk), :]
    pending = lax.fori_loop(0, n_chunks - 1, body, tuple(scores(hh, k_first) for hh in range(SKEW)),
                            unroll=4)
    last = FRONT + (n_chunks - 1) * tk
    chunk(k_ref[pl.ds(last, tk), :], with_ones(v_ref[pl.ds(last, tk), :]), pending, None)
    rowi = lax.broadcasted_iota(jnp.int32, (QB, DH), 0)
    keep_rows = jnp.logical_or(qi > 0, rowi >= META0)
    for hh in range(GROUP):
        o_ref[:, hh * DH:(hh + 1) * DH] = jnp.where(keep_rows, acc_ref[hh] / l_ref[hh], 0.0).astype(bf16)


def _flash(qkv, B, Lp):
    R = qkv.shape[0]
    S = Lp - FRONT
    tk = 512 if S % 512 == 0 else 256
    nq = Lp // QB
    kcol, vcol = D // DH, (D + KV_DIM) // DH
    return pl.pallas_call(
        functools.partial(_flash_kernel, n_chunks=S // tk, tk=tk),
        grid=(B, N_KV, nq),
        in_specs=[
            pl.BlockSpec((QB, QW), lambda b, g, i: (b * nq + i, g)),
            pl.BlockSpec((Lp, DH), lambda b, g, i: (b, kcol + g)),
            pl.BlockSpec((Lp, DH), lambda b, g, i: (b, vcol + g)),
        ],
        out_specs=pl.BlockSpec((QB, QW), lambda b, g, i: (b * nq + i, g)),
        out_shape=jax.ShapeDtypeStruct((R, D), bf16),
        scratch_shapes=[pltpu.VMEM((GROUP, QB, DH), f32)] * 3,
        compiler_params=_cparams(("parallel", "parallel", "arbitrary")),
        name="dense_attention",
    )(qkv, qkv, qkv)


def _win_kernel(hs_ref, q_ref, k_ref, v_ref, o_ref, *, Lp):
    g = pl.program_id(1)
    qi = pl.program_id(2)
    band = 2 * QB
    start = pl.multiple_of(jnp.clip(QB * qi - WINDOW, 0, Lp - band), WINDOW)
    kb = k_ref[pl.ds(start, band), :]
    vb = v_ref[pl.ds(start, band), :]
    km = k_ref[FRONT - DH:FRONT, :]
    vm = v_ref[FRONT - DH:FRONT, :]
    meta_ok = lax.broadcasted_iota(jnp.int32, (QB, DH), 1) >= DH - N_META
    qpos = QB * qi + lax.broadcasted_iota(jnp.int32, (QB, band), 0)
    kpos = start + lax.broadcasted_iota(jnp.int32, (QB, band), 1)
    dist = jnp.abs(qpos - kpos)
    ok = jnp.logical_and(dist <= WINDOW, kpos >= FRONT)
    distf = dist.astype(f32)
    rowi = lax.broadcasted_iota(jnp.int32, (QB, DH), 0)
    keep_rows = jnp.logical_or(qi > 0, rowi >= META0)

    def scores(hh):
        q = q_ref[:, hh * DH:(hh + 1) * DH]
        return _dot_nt(q, kb), _dot_nt(q, km)

    nxt = scores(0)
    for hh in range(GROUP):
        raw_b, raw_m = nxt
        if hh + 1 < GROUP:
            nxt = scores(hh + 1)
        slope = jnp.where(qi == 0, 0.0, hs_ref[0, g * GROUP + hh])
        sink = hs_ref[1, g * GROUP + hh]
        sb = jnp.where(ok, raw_b - slope * distf, NEG)
        sm = jnp.where(meta_ok, raw_m, NEG)
        m = jnp.maximum(jnp.max(sb, axis=-1, keepdims=True), jnp.max(sm, axis=-1, keepdims=True))
        m = jnp.maximum(m, sink)
        eb = jnp.exp2(sb - m)
        em = jnp.exp2(sm - m)
        den = (jnp.sum(eb, axis=-1, keepdims=True) + jnp.sum(em, axis=-1, keepdims=True)
               + jnp.exp2(sink - m))
        acc = _dot(eb.astype(bf16), vb) + _dot(em.astype(bf16), vm)
        o_ref[:, hh * DH:(hh + 1) * DH] = jnp.where(keep_rows, acc / den, 0.0).astype(bf16)


def _win_attention(qkv, sink, B, Lp):
    R = qkv.shape[0]
    nq = Lp // QB
    kcol, vcol = D // DH, (D + KV_DIM) // DH
    slopes = 2.0 ** (-8.0 * jnp.arange(1, N_HEADS + 1, dtype=f32) / N_HEADS)
    hs = jnp.stack([slopes, sink.astype(f32)]) * LOG2E
    return pl.pallas_call(
        functools.partial(_win_kernel, Lp=Lp),
        grid=(B, N_KV, nq),
        in_specs=[
            pl.BlockSpec(memory_space=pltpu.SMEM),
            pl.BlockSpec((QB, QW), lambda b, g, i: (b * nq + i, g)),
            pl.BlockSpec((Lp, DH), lambda b, g, i: (b, kcol + g)),
            pl.BlockSpec((Lp, DH), lambda b, g, i: (b, vcol + g)),
        ],
        out_specs=pl.BlockSpec((QB, QW), lambda b, g, i: (b * nq + i, g)),
        out_shape=jax.ShapeDtypeStruct((R, D), bf16),
        compiler_params=_cparams(("parallel", "parallel", "arbitrary")),
        name="window_attention",
    )(hs, qkv, qkv, qkv)


TF = 512


def _ffn_kernel(*refs, n_h, n_a, nblk, f_axis):
    refs = list(refs)
    h_refs = [refs.pop(0) for _ in range(n_h)]
    front_ref = refs.pop(0) if nblk else None
    a_refs = [refs.pop(0) for _ in range(n_a)]
    wp_ref = refs.pop(0) if n_a else None
    g_ref, wg_ref, wu_ref, wo_ref, o_ref, xn_ref = refs

    @pl.when(pl.program_id(f_axis) == 0)
    def _():
        x = _tile_rows(pl.program_id(0), h_refs, front_ref, nblk)
        if n_a:
            x = x + _dot(_tile_rows(0, a_refs, None, 0), wp_ref[...])
        xn_ref[...] = _rms(x, g_ref[...]).astype(bf16)
        o_ref[...] = x

    xn = xn_ref[...]
    gate = _dot(xn, wg_ref[...])
    up = _dot(xn, wu_ref[...])
    a = gate * jax.nn.sigmoid(gate) * up
    o_ref[...] += _dot(a.astype(bf16), wo_ref[...])


def _ffn(h, g, w_in, w_out, mix=None, h0=None, real=None):
    nf = D_FF // TF
    nblk = sblk = 0
    prow = TM
    if h0 is not None or real is not None:
        B, S = (h0 or real)[:2]
        nblk, sblk = (S + FRONT) // FRONT, S // FRONT
        prow = FRONT
    npc = TM // prow
    if real is None:
        R = h.shape[0] if h0 is None else B * (S + FRONT)
        grid = (R // TM, nf)
        out_spec = pl.BlockSpec((TM, D), lambda i, f: (i, 0))
        out_shape = jax.ShapeDtypeStruct((R, D), f32)
    else:
        grid = (B, S // TM, nf)
        out_spec = pl.BlockSpec((None, TM, D), lambda b, j, f: (b, j, 0))
        out_shape = jax.ShapeDtypeStruct((B, S, D), f32)

    def const(*block):
        return lambda *idx: block

    def piece(k):
        if real is None:
            return lambda i, f: (npc * i + k, 0)
        return lambda b, j, f: (b * nblk + 1 + npc * j + k, 0)

    if h0 is not None:
        in_specs = [pl.BlockSpec((FRONT, D), functools.partial(
            lambda i, f, k: (_h0_index(npc * i + k, nblk, sblk), 0), k=k)) for k in range(npc)]
        in_specs.append(pl.BlockSpec((FRONT, D), const(0, 0)))
        args = [h] * npc + [h0[2]]
    else:
        in_specs = [pl.BlockSpec((prow, D), piece(k)) for k in range(npc)]
        args = [h] * npc
    n_a = 0
    if mix is not None:
        n_a = npc
        in_specs += [pl.BlockSpec((prow, D), piece(k)) for k in range(npc)]
        in_specs.append(pl.BlockSpec((D, D), const(0, 0), pipeline_mode=pl.Buffered(1)))
        args += [mix[0]] * npc + [mix[1]]
    in_specs += [
        pl.BlockSpec((1, D), const(0, 0)),
        pl.BlockSpec((D, TF), lambda *idx: (0, idx[-1])),
        pl.BlockSpec((D, TF), lambda *idx: (0, nf + idx[-1])),
        pl.BlockSpec((TF, D), lambda *idx: (idx[-1], 0)),
    ]
    args += [g.reshape(1, D), w_in, w_in, w_out]
    return pl.pallas_call(
        functools.partial(_ffn_kernel, n_h=npc, n_a=n_a, nblk=nblk if h0 is not None else 0,
                          f_axis=len(grid) - 1),
        grid=grid,
        in_specs=in_specs,
        out_specs=out_spec,
        out_shape=out_shape,
        scratch_shapes=[pltpu.VMEM((TM, D), bf16)],
        compiler_params=_cparams(("parallel",) * (len(grid) - 1) + ("arbitrary",)),
        name="swiglu_ffn",
    )(*args)


TC = 512


def _conv_kernel(h_ref, hp_ref, hn_ref, g_ref, wb_ref, wc_ref, wu_ref, cw_ref, wo_ref, o_ref,
                 xn_ref, gu_ref):
    i = pl.program_id(0)

    @pl.when(pl.program_id(1) == 0)
    def _():
        x = h_ref[...]
        gain = g_ref[...]
        xn_ref[HALO:HALO + TM, :] = _rms(x, gain).astype(bf16)
        xn_ref[0:HALO, :] = _rms(hp_ref[...], gain).astype(bf16)
        nxt = _rms(hn_ref[...], gain)
        nxt = jnp.where(i == pl.num_programs(0) - 1, 0.0, nxt)
        xn_ref[HALO + TM:, :] = nxt.astype(bf16)
        o_ref[...] = x

    xn = xn_ref[...]
    gu_ref[...] = _dot(xn, wc_ref[...]) * _dot(xn, wu_ref[...])
    cw = cw_ref[...]
    conv = (gu_ref[HALO - 1:HALO - 1 + TM, :] * cw[0:1, :]
            + gu_ref[HALO:HALO + TM, :] * cw[1:2, :]
            + gu_ref[HALO + 1:HALO + 1 + TM, :] * cw[2:3, :])
    bgate = _dot(xn_ref[HALO:HALO + TM, :], wb_ref[...])
    o_ref[...] += _dot((bgate * conv).astype(bf16), wo_ref[...])


def _conv_mixer(h, g, w_in, conv_w, w_out):
    R = h.shape[0]
    nc = D // TC
    hb = TM // HALO
    last = R // HALO - 1
    return pl.pallas_call(
        _conv_kernel,
        grid=(R // TM, nc),
        in_specs=[
            pl.BlockSpec((TM, D), lambda i, c: (i, 0)),
            pl.BlockSpec((HALO, D), lambda i, c: (jnp.maximum(i * hb - 1, 0), 0)),
            pl.BlockSpec((HALO, D), lambda i, c: (jnp.minimum((i + 1) * hb, last), 0)),
            pl.BlockSpec((1, D), lambda i, c: (0, 0)),
            pl.BlockSpec((D, TC), lambda i, c: (0, c)),
            pl.BlockSpec((D, TC), lambda i, c: (0, nc + c)),
            pl.BlockSpec((D, TC), lambda i, c: (0, 2 * nc + c)),
            pl.BlockSpec((3, TC), lambda i, c: (0, c)),
            pl.BlockSpec((TC, D), lambda i, c: (c, 0)),
        ],
        out_specs=pl.BlockSpec((TM, D), lambda i, c: (i, 0)),
        out_shape=jax.ShapeDtypeStruct((R, D), f32),
        scratch_shapes=[pltpu.VMEM((TM + 2 * HALO, D), bf16), pltpu.VMEM((TM + 2 * HALO, TC), f32)],
        compiler_params=_cparams(("parallel", "arbitrary")),
        name="gated_conv",
    )(h, h, h, g.reshape(1, D), w_in, w_in, w_in, conv_w, w_out)


NJ = 16


def _dft_factors(L):
    best = None
    for l1 in range(16, min(L, 128) + 1, 16):
        if L % l1:
            continue
        l2p = -(-(L // l1) // NJ) * NJ
        cost = 4 * l1 * l1 * NJ * l2p + 8 * l2p * 256 * -(-l2p // 256) * l1
        if best is None or cost < best[0]:
            best = (cost, l1, L // l1, l2p)
    return best[1:]


def _dft_tables(L, L1, L2, L2p, shift):
    def cs(num, den):
        ang = (2.0 * math.pi / den) * (num % den).astype(f32)
        return jnp.cos(ang), jnp.sin(ang)

    i1 = jnp.arange(L1)
    c1, s1 = cs(i1[:, None] * i1[None, :], L1)
    eye = jnp.eye(NJ, dtype=f32)
    w1 = jnp.concatenate([jnp.kron(c1, eye), jnp.kron(-s1, eye)], axis=0).astype(bf16)
    n2 = jnp.arange(L2p)
    tc, ts = cs(i1[:, None] * n2[None, :], L)
    live = (n2 < L2)[None, :] * (L ** -0.5)

    def block_rows(t):
        return (t * live).reshape(L1, L2p // NJ, NJ).transpose(1, 0, 2).reshape(L2p // NJ, L1 * NJ, 1)

    tw = jnp.stack([block_rows(tc), block_rows(ts)], axis=1)
    k2 = n2 - shift
    c3, s3 = cs(k2[:, None] * n2[None, :], L2)
    inside = jnp.logical_and(jnp.logical_and(k2[:, None] >= 0, k2[:, None] < L2), n2[None, :] < L2)
    w3 = jnp.stack([jnp.where(inside, c3, 0.0), jnp.where(inside, s3, 0.0)]).astype(bf16)
    ic = jnp.arange(FGROUP)
    cc, sc = cs(ic[:, None] * ic[None, :], FGROUP)
    wc = (jnp.stack([cc, sc]) * (FGROUP ** -0.5)).astype(bf16)
    return w1, tw, w3, wc


def _dft1_kernel(x_ref, w_ref, tw_ref, tr_ref, ti_ref):
    l1, nj, dc = x_ref.shape
    rows = l1 * nj
    t = _dot(w_ref[...], x_ref[...].reshape(rows, dc))
    tr, ti = t[:rows], t[rows:]
    tc, ts = tw_ref[0], tw_ref[1]
    tr_ref[...] = (tr * tc + ti * ts).astype(bf16).reshape(l1, nj, dc)
    ti_ref[...] = (ti * tc - tr * ts).astype(bf16).reshape(l1, nj, dc)


def _dft3_kernel(tr_ref, ti_ref, w3_ref, wc_ref, o_ref):
    c3, s3 = w3_ref[0], w3_ref[1]
    tr, ti = tr_ref[...], ti_ref[...]
    yr = (_dot(c3, tr) + _dot(s3, ti)).astype(bf16)
    yi = (_dot(c3, ti) - _dot(s3, tr)).astype(bf16)
    cc, sc = wc_ref[0], wc_ref[1]
    for gidx in range(o_ref.shape[1] // FGROUP):
        sl = slice(gidx * FGROUP, (gidx + 1) * FGROUP)
        o_ref[:, sl] = (_dot(yr[:, sl], cc) + _dot(yi[:, sl], sc)).astype(bf16)


def _fourier_mix(hn, B, Lp):
    L = Lp - META0
    L1, L2, L2p = _dft_factors(L)
    direct = META0 % L1 == 0 and L1 * L2p == Lp
    shift = META0 // L1 if direct else 0
    w1, tw, w3, wc = _dft_tables(L, L1, L2, L2p, shift)
    x = hn.reshape(B, Lp, D)[:, META0:].reshape(B, L1, L2, D)
    x = jnp.pad(x, ((0, 0), (0, 0), (0, L2p - L2), (0, 0)))
    dc1 = 512
    rows = L1 * NJ
    t_shape = jax.ShapeDtypeStruct((B, L1, L2p, D), bf16)
    t_spec = pl.BlockSpec((None, L1, NJ, dc1), lambda b, n, c: (b, 0, n, c))
    tr, ti = pl.pallas_call(
        _dft1_kernel,
        grid=(B, L2p // NJ, D // dc1),
        in_specs=[
            t_spec,
            pl.BlockSpec((2 * rows, rows), lambda b, n, c: (0, 0)),
            pl.BlockSpec((None, 2, rows, 1), lambda b, n, c: (n, 0, 0, 0)),
        ],
        out_specs=[t_spec, t_spec],
        out_shape=[t_shape, t_shape],
        compiler_params=_cparams(("parallel", "parallel", "arbitrary")),
        name="seq_dft_stage1",
    )(x, w1, tw)
    dc3 = D if L2p <= 256 else 512
    nc3 = D // dc3
    slab = pl.BlockSpec((None, None, L2p, dc3), lambda b, k, c: (b, k, 0, c))
    mixed = pl.pallas_call(
        _dft3_kernel,
        grid=(B, L1, nc3),
        in_specs=[
            slab,
            slab,
            pl.BlockSpec((2, L2p, L2p), lambda b, k, c: (0, 0, 0)),
            pl.BlockSpec((2, FGROUP, FGROUP), lambda b, k, c: (0, 0, 0)),
        ],
        out_specs=pl.BlockSpec((None, L2p, dc3), lambda b, k, c: (b, 0, k * nc3 + c)),
        out_shape=jax.ShapeDtypeStruct((B, L2p, L1 * D), bf16),
        compiler_params=_cparams(("parallel", "parallel", "arbitrary")),
        name="seq_dft_stage2_channel_dft",
    )(tr, ti, w3, wc)
    if direct:
        return mixed.reshape(B * Lp, D)
    mixed = mixed.reshape(B, L2p * L1, D)[:, :L]
    return jnp.pad(mixed, ((0, 0), (META0, 0), (0, 0))).reshape(B * Lp, D)


def _encode(x, meta, p):
    B, S, _ = x.shape
    Lp = S + FRONT
    front = jnp.concatenate([jnp.zeros((META0, D), f32), meta.astype(f32)], axis=0)
    h0 = (B, S, front)
    x2d = x.reshape(B * S, D)

    qkv = _qkv(x2d, p["ln_mix"][0], *_rope_permute(p["a_wqkv"], p["a_q_norm"], p["a_k_norm"]),
               _rope_tables(S, B), h0=h0)
    h = _ffn(x2d, p["ln_ffn"][0], p["ffn_w_in"][0], p["ffn_w_out"][0], (_flash(qkv, B, Lp), p["a_wo"]),
             h0=h0)
    h = _ffn(h, p["ln_ffn"][1], p["ffn_w_in"][1], p["ffn_w_out"][1],
             (_fourier_mix(_norm(h, p["ln_mix"][1]), B, Lp), p["b_w"]))
    h = _conv_mixer(h, p["ln_mix"][2], p["c_w_in"], p["c_conv"], p["c_w_out"])
    h = _ffn(h, p["ln_ffn"][2], p["ffn_w_in"][2], p["ffn_w_out"][2])
    qkv = _qkv(h, p["ln_mix"][3], p["d_wqkv"], p["d_q_norm"], p["d_k_norm"], None)
    return _ffn(h, p["ln_ffn"][3], p["ffn_w_in"][3], p["ffn_w_out"][3],
                (_win_attention(qkv, p["d_sink"], B, Lp), p["d_wo"]), real=(B, S))


def kernel(x_prompt, x_sample, meta_tokens, ln_mix, ln_ffn, a_wqkv, a_q_norm, a_k_norm, a_wo, b_w,
           c_w_in, c_conv, c_w_out, d_wqkv, d_q_norm, d_k_norm, d_sink, d_wo, ffn_w_in, ffn_w_out):
    p = {
        "ln_mix": ln_mix, "ln_ffn": ln_ffn,
        "a_wqkv": a_wqkv[0].astype(bf16), "a_q_norm": a_q_norm[0], "a_k_norm": a_k_norm[0],
        "a_wo": a_wo[0].astype(bf16), "b_w": b_w[0].astype(bf16),
        "c_w_in": c_w_in[0].astype(bf16), "c_conv": c_conv[0], "c_w_out": c_w_out[0].astype(bf16),
        "d_wqkv": d_wqkv[0].astype(bf16), "d_q_norm": d_q_norm[0], "d_k_norm": d_k_norm[0],
        "d_sink": d_sink[0], "d_wo": d_wo[0].astype(bf16),
        "ffn_w_in": [ffn_w_in[i].astype(bf16) for i in range(ffn_w_in.shape[0])],
        "ffn_w_out": [ffn_w_out[i].astype(bf16) for i in range(ffn_w_out.shape[0])],
    }
    return _encode(x_prompt, meta_tokens, p), _encode(x_sample, meta_tokens, p)
```

```python
import functools
import math

import numpy as np
import jax
import jax.numpy as jnp
from jax import lax
from jax.experimental import pallas as pl
from jax.experimental.pallas import tpu as pltpu

D = 2048
N_HEADS = 16
N_KV = 4
GROUP = N_HEADS // N_KV
DH = D // N_HEADS
QW = GROUP * DH
KV_DIM = N_KV * DH
QKV_DIM = D + 2 * KV_DIM
WINDOW = 128
ROPE_THETA = 10000.0
GRID_W = 64
FGROUP = 256
D_FF = 5632
N_META = 16
EPS = 1e-6
NEG = -1e30
LOG2E = 1.4426950408889634
Q_SCALE = DH ** -0.5 * LOG2E

FRONT = 256
META0 = FRONT - N_META
QB = 256
TM = 512
HALO = 16
SKEW = 1
VMEM_LIMIT = 52 * 1024 * 1024

bf16 = jnp.bfloat16
f32 = jnp.float32


def _cparams(sem):
    return pltpu.CompilerParams(dimension_semantics=sem, vmem_limit_bytes=VMEM_LIMIT)


def _rms(x, g):
    return x * lax.rsqrt(jnp.mean(x * x, axis=-1, keepdims=True) + EPS) * g


def _dot(a, b):
    return jnp.dot(a, b, preferred_element_type=f32)


def _dot_nt(a, b):
    return lax.dot_general(a, b, (((1,), (1,)), ((), ())), preferred_element_type=f32)


def _h0_index(t, nblk, sblk):
    return (t // nblk) * sblk + jnp.maximum(t % nblk - 1, 0)


def _tile_rows(i, piece_refs, front_ref, nblk):
    pieces = [r[...] for r in piece_refs]
    if front_ref is not None:
        n = len(pieces)
        pieces = [jnp.where(lax.rem(n * i + k, nblk) == 0, front_ref[...], pc) for k, pc in enumerate(pieces)]
    return pieces[0] if len(pieces) == 1 else jnp.concatenate(pieces, axis=0)


def _norm_kernel(h_ref, g_ref, o_ref):
    o_ref[...] = _rms(h_ref[...], g_ref[...]).astype(bf16)


def _norm(h, g):
    R = h.shape[0]
    return pl.pallas_call(
        _norm_kernel,
        grid=(R // TM,),
        in_specs=[pl.BlockSpec((TM, D), lambda i: (i, 0)), pl.BlockSpec((1, D), lambda i: (0, 0))],
        out_specs=pl.BlockSpec((TM, D), lambda i: (i, 0)),
        out_shape=jax.ShapeDtypeStruct((R, D), bf16),
        compiler_params=_cparams(("parallel",)),
        name="rms_norm",
    )(h, g.reshape(1, D))


def _head_norm(yh, gain):
    return yh * lax.rsqrt(jnp.mean(yh * yh, axis=-1, keepdims=True) + EPS) * gain


def _qkv_kernel(*refs, rope, n_h, nblk):
    refs = list(refs)
    h_refs = [refs.pop(0) for _ in range(n_h)]
    front_ref = refs.pop(0) if nblk else None
    if rope:
        g_ref, w_ref, hg_ref, cos_ref, sin_ref, o_ref, xn_ref = refs
    else:
        g_ref, w_ref, hg_ref, o_ref, xn_ref = refs
    h = _tile_rows(pl.program_id(0), h_refs, front_ref, nblk)
    xn_ref[...] = _rms(h, g_ref[...]).astype(bf16)
    n_q = D // QW

    def project(j):
        return _dot(xn_ref[...], w_ref[:, j * QW:(j + 1) * QW])

    def finish(j, y):
        if j > n_q:
            o_ref[:, j * QW:(j + 1) * QW] = y.astype(bf16)
            return
        gain = hg_ref[0:1, :] if j < n_q else hg_ref[1:2, :]
        scale = Q_SCALE if j < n_q else 1.0
        for hh in range(GROUP):
            n = _head_norm(y[:, hh * DH:(hh + 1) * DH], gain)
            if rope:
                n = n * cos_ref[...] + pltpu.roll(n, DH // 2, 1) * sin_ref[...]
            o_ref[:, j * QW + hh * DH:j * QW + (hh + 1) * DH] = (n * scale).astype(bf16)

    n_tiles = QKV_DIM // QW
    y = project(0)
    for j in range(n_tiles):
        y_next = project(j + 1) if j + 1 < n_tiles else None
        finish(j, y)
        y = y_next


def _qkv(h, g, w, q_gain, k_gain, rope_tabs, h0=None):
    rope = rope_tabs is not None
    if h0 is None:
        R, nblk = h.shape[0], 0
        in_specs, args = [pl.BlockSpec((TM, D), lambda i: (i, 0))], [h]
    else:
        B, S, front = h0
        R, nblk, sblk, npc = B * (S + FRONT), (S + FRONT) // FRONT, S // FRONT, TM // FRONT
        in_specs = [pl.BlockSpec((FRONT, D), functools.partial(
            lambda i, k: (_h0_index(npc * i + k, nblk, sblk), 0), k=k)) for k in range(npc)]
        in_specs.append(pl.BlockSpec((FRONT, D), lambda i: (0, 0)))
        args = [h] * npc + [front]
    n_h = len(args) - (nblk > 0)
    in_specs += [
        pl.BlockSpec((1, D), lambda i: (0, 0)),
        pl.BlockSpec((D, QKV_DIM), lambda i: (0, 0)),
        pl.BlockSpec((2, DH), lambda i: (0, 0)),
    ]
    args += [g.reshape(1, D), w, jnp.stack([q_gain, k_gain])]
    if rope:
        in_specs += [pl.BlockSpec((TM, DH), lambda i: (i, 0))] * 2
        args += list(rope_tabs)
    return pl.pallas_call(
        functools.partial(_qkv_kernel, rope=rope, n_h=n_h, nblk=nblk),
        grid=(R // TM,),
        in_specs=in_specs,
        out_specs=pl.BlockSpec((TM, QKV_DIM), lambda i: (i, 0)),
        out_shape=jax.ShapeDtypeStruct((R, QKV_DIM), bf16),
        scratch_shapes=[pltpu.VMEM((TM, D), bf16)],
        compiler_params=_cparams(("parallel",)),
        name="qkv_proj",
    )(*args)


ROPE_PERM = np.concatenate([np.arange(0, 32), np.arange(64, 96), np.arange(32, 64), np.arange(96, 128)])


def _rope_permute(wqkv, q_gain, k_gain):
    n_qk = N_HEADS + N_KV
    cols = (np.arange(n_qk)[:, None] * DH + ROPE_PERM[None, :]).reshape(-1)
    cols = np.concatenate([cols, np.arange(n_qk * DH, QKV_DIM)])
    return wqkv[:, cols], q_gain[ROPE_PERM], k_gain[ROPE_PERM]


def _rope_tables(S, B):
    quarter = DH // 4
    inv = ROPE_THETA ** (-jnp.arange(quarter, dtype=f32) / quarter)
    t = jnp.arange(S)
    row = jnp.concatenate([jnp.zeros((FRONT,), f32), (t // GRID_W).astype(f32)])
    col = jnp.concatenate([jnp.zeros((FRONT,), f32), (t % GRID_W).astype(f32)])
    ang_r = row[:, None] * inv[None, :]
    ang_c = col[:, None] * inv[None, :]
    ang = jnp.concatenate([ang_r, ang_c, ang_r, ang_c], axis=1)
    cos, sin = jnp.cos(ang), jnp.sin(ang)
    sin = jnp.where((np.arange(DH) < DH // 2)[None, :], -sin, sin)
    return jnp.tile(cos, (B, 1)), jnp.tile(sin, (B, 1))


def _flash_kernel(q_ref, k_ref, v_ref, o_ref, m_ref, l_ref, acc_ref, *, n_chunks, tk):
    qi = pl.program_id(2)
    col0 = lax.broadcasted_iota(jnp.int32, (QB, DH), 1) + (FRONT - DH)

    def with_ones(v):
        return jnp.concatenate([v, jnp.ones(v.shape, bf16)], axis=1)

    k0 = k_ref[FRONT - DH:FRONT, :]
    v0 = with_ones(v_ref[FRONT - DH:FRONT, :])
    for hh in range(GROUP):
        s = jnp.where(col0 >= META0, _dot_nt(q_ref[:, hh * DH:(hh + 1) * DH], k0), NEG)
        m = jnp.max(s, axis=-1, keepdims=True)
        pv = _dot(jnp.exp2(s - m).astype(bf16), v0)
        m_ref[hh] = jnp.broadcast_to(m, (QB, DH))
        l_ref[hh] = pv[:, DH:]
        acc_ref[hh] = pv[:, :DH]

    def scores(hh, kc):
        return _dot_nt(q_ref[:, hh * DH:(hh + 1) * DH], kc)

    def update(hh, s, vc):
        m_prev = m_ref[hh]
        m_new = jnp.maximum(m_prev, jnp.max(s, axis=-1, keepdims=True))
        p = jnp.exp2((s - jnp.concatenate([m_new] * (tk // DH), axis=1)).astype(bf16))
        alpha = jnp.exp2(m_prev - m_new)
        pv = _dot(p, vc)
        l_ref[hh] = alpha * l_ref[hh] + pv[:, DH:]
        acc_ref[hh] = alpha * acc_ref[hh] + pv[:, :DH]
        m_ref[hh] = m_new

    def chunk(kc, vc, pending, k_next):
        pending = list(pending)
        for hh in range(GROUP):
            ahead = hh + SKEW
            if ahead < GROUP:
                pending.append(scores(ahead, kc))
            elif k_next is not None:
                pending.append(scores(ahead - GROUP, k_next))
            update(hh, pending.pop(0), vc)
        return tuple(pending)

    def body(c, pending):
        start = pl.multiple_of(FRONT + c * tk, FRONT)
        return chunk(k_ref[pl.ds(start, tk), :], with_ones(v_ref[pl.ds(start, tk), :]), pending,
                     k_ref[pl.ds(start + tk, tk), :])

    k_first = k_ref[pl.ds(FRONT, tk), :]
    pending = lax.fori_loop(0, n_chunks - 1, body, tuple(scores(hh, k_first) for hh in range(SKEW)),
                            unroll=8)
    last = FRONT + (n_chunks - 1) * tk
    chunk(k_ref[pl.ds(last, tk), :], with_ones(v_ref[pl.ds(last, tk), :]), pending, None)
    rowi = lax.broadcasted_iota(jnp.int32, (QB, DH), 0)
    keep_rows = jnp.logical_or(qi > 0, rowi >= META0)
    for hh in range(GROUP):
        o_ref[:, hh * DH:(hh + 1) * DH] = jnp.where(keep_rows, acc_ref[hh] / l_ref[hh], 0.0).astype(bf16)


def _flash(qkv, B, Lp):
    R = qkv.shape[0]
    S = Lp - FRONT
    tk = 512 if S % 512 == 0 else 256
    nq = Lp // QB
    kcol, vcol = D // DH, (D + KV_DIM) // DH
    return pl.pallas_call(
        functools.partial(_flash_kernel, n_chunks=S // tk, tk=tk),
        grid=(B, N_KV, nq),
        in_specs=[
            pl.BlockSpec((QB, QW), lambda b, g, i: (b * nq + i, g)),
            pl.BlockSpec((Lp, DH), lambda b, g, i: (b, kcol + g)),
            pl.BlockSpec((Lp, DH), lambda b, g, i: (b, vcol + g)),
        ],
        out_specs=pl.BlockSpec((QB, QW), lambda b, g, i: (b * nq + i, g)),
        out_shape=jax.ShapeDtypeStruct((R, D), bf16),
        scratch_shapes=[pltpu.VMEM((GROUP, QB, DH), f32)] * 3,
        compiler_params=_cparams(("parallel", "parallel", "arbitrary")),
        name="dense_attention",
    )(qkv, qkv, qkv)


def _win_kernel(hs_ref, q_ref, k_ref, v_ref, o_ref, *, Lp):
    g = pl.program_id(1)
    qi = pl.program_id(2)
    band = 2 * QB
    start = pl.multiple_of(jnp.clip(QB * qi - WINDOW, 0, Lp - band), WINDOW)
    kb = k_ref[pl.ds(start, band), :]
    vb = v_ref[pl.ds(start, band), :]
    km = k_ref[FRONT - DH:FRONT, :]
    vm = v_ref[FRONT - DH:FRONT, :]
    meta_ok = lax.broadcasted_iota(jnp.int32, (QB, DH), 1) >= DH - N_META
    qpos = QB * qi + lax.broadcasted_iota(jnp.int32, (QB, band), 0)
    kpos = start + lax.broadcasted_iota(jnp.int32, (QB, band), 1)
    dist = jnp.abs(qpos - kpos)
    ok = jnp.logical_and(dist <= WINDOW, kpos >= FRONT)
    distf = dist.astype(f32)
    rowi = lax.broadcasted_iota(jnp.int32, (QB, DH), 0)
    keep_rows = jnp.logical_or(qi > 0, rowi >= META0)

    def scores(hh):
        q = q_ref[:, hh * DH:(hh + 1) * DH]
        return _dot_nt(q, kb), _dot_nt(q, km)

    nxt = scores(0)
    for hh in range(GROUP):
        raw_b, raw_m = nxt
        if hh + 1 < GROUP:
            nxt = scores(hh + 1)
        slope = jnp.where(qi == 0, 0.0, hs_ref[0, g * GROUP + hh])
        sink = hs_ref[1, g * GROUP + hh]
        sb = jnp.where(ok, raw_b - slope * distf, NEG)
        sm = jnp.where(meta_ok, raw_m, NEG)
        m = jnp.maximum(jnp.max(sb, axis=-1, keepdims=True), jnp.max(sm, axis=-1, keepdims=True))
        m = jnp.maximum(m, sink)
        eb = jnp.exp2(sb - m)
        em = jnp.exp2(sm - m)
        den = (jnp.sum(eb, axis=-1, keepdims=True) + jnp.sum(em, axis=-1, keepdims=True)
               + jnp.exp2(sink - m))
        acc = _dot(eb.astype(bf16), vb) + _dot(em.astype(bf16), vm)
        o_ref[:, hh * DH:(hh + 1) * DH] = jnp.where(keep_rows, acc / den, 0.0).astype(bf16)


def _win_attention(qkv, sink, B, Lp):
    R = qkv.shape[0]
    nq = Lp // QB
    kcol, vcol = D // DH, (D + KV_DIM) // DH
    slopes = 2.0 ** (-8.0 * jnp.arange(1, N_HEADS + 1, dtype=f32) / N_HEADS)
    hs = jnp.stack([slopes, sink.astype(f32)]) * LOG2E
    return pl.pallas_call(
        functools.partial(_win_kernel, Lp=Lp),
        grid=(B, N_KV, nq),
        in_specs=[
            pl.BlockSpec(memory_space=pltpu.SMEM),
            pl.BlockSpec((QB, QW), lambda b, g, i: (b * nq + i, g)),
            pl.BlockSpec((Lp, DH), lambda b, g, i: (b, kcol + g)),
            pl.BlockSpec((Lp, DH), lambda b, g, i: (b, vcol + g)),
        ],
        out_specs=pl.BlockSpec((QB, QW), lambda b, g, i: (b * nq + i, g)),
        out_shape=jax.ShapeDtypeStruct((R, D), bf16),
        compiler_params=_cparams(("parallel", "parallel", "arbitrary")),
        name="window_attention",
    )(hs, qkv, qkv, qkv)


TF = 512


def _ffn_kernel(*refs, n_h, n_a, nblk, f_axis):
    refs = list(refs)
    h_refs = [refs.pop(0) for _ in range(n_h)]
    front_ref = refs.pop(0) if nblk else None
    a_refs = [refs.pop(0) for _ in range(n_a)]
    wp_ref = refs.pop(0) if n_a else None
    g_ref, wg_ref, wu_ref, wo_ref, o_ref, xn_ref = refs

    @pl.when(pl.program_id(f_axis) == 0)
    def _():
        x = _tile_rows(pl.program_id(0), h_refs, front_ref, nblk)
        if n_a:
            x = x + _dot(_tile_rows(0, a_refs, None, 0), wp_ref[...])
        xn_ref[...] = _rms(x, g_ref[...]).astype(bf16)
        o_ref[...] = x

    xn = xn_ref[...]
    gate = _dot(xn, wg_ref[...])
    up = _dot(xn, wu_ref[...])
    a = gate * jax.nn.sigmoid(gate) * up
    o_ref[...] += _dot(a.astype(bf16), wo_ref[...])


def _ffn(h, g, w_in, w_out, mix=None, h0=None, real=None):
    nf = D_FF // TF
    nblk = sblk = 0
    prow = TM
    if h0 is not None or real is not None:
        B, S = (h0 or real)[:2]
        nblk, sblk = (S + FRONT) // FRONT, S // FRONT
        prow = FRONT
    npc = TM // prow
    if real is None:
        R = h.shape[0] if h0 is None else B * (S + FRONT)
        grid = (R // TM, nf)
        out_spec = pl.BlockSpec((TM, D), lambda i, f: (i, 0))
        out_shape = jax.ShapeDtypeStruct((R, D), f32)
    else:
        grid = (B, S // TM, nf)
        out_spec = pl.BlockSpec((None, TM, D), lambda b, j, f: (b, j, 0))
        out_shape = jax.ShapeDtypeStruct((B, S, D), f32)

    def const(*block):
        return lambda *idx: block

    def piece(k):
        if real is None:
            return lambda i, f: (npc * i + k, 0)
        return lambda b, j, f: (b * nblk + 1 + npc * j + k, 0)

    if h0 is not None:
        in_specs = [pl.BlockSpec((FRONT, D), functools.partial(
            lambda i, f, k: (_h0_index(npc * i + k, nblk, sblk), 0), k=k)) for k in range(npc)]
        in_specs.append(pl.BlockSpec((FRONT, D), const(0, 0)))
        args = [h] * npc + [h0[2]]
    else:
        in_specs = [pl.BlockSpec((prow, D), piece(k)) for k in range(npc)]
        args = [h] * npc
    n_a = 0
    if mix is not None:
        n_a = npc
        in_specs += [pl.BlockSpec((prow, D), piece(k)) for k in range(npc)]
        in_specs.append(pl.BlockSpec((D, D), const(0, 0), pipeline_mode=pl.Buffered(1)))
        args += [mix[0]] * npc + [mix[1]]
    in_specs += [
        pl.BlockSpec((1, D), const(0, 0)),
        pl.BlockSpec((D, TF), lambda *idx: (0, idx[-1])),
        pl.BlockSpec((D, TF), lambda *idx: (0, nf + idx[-1])),
        pl.BlockSpec((TF, D), lambda *idx: (idx[-1], 0)),
    ]
    args += [g.reshape(1, D), w_in, w_in, w_out]
    return pl.pallas_call(
        functools.partial(_ffn_kernel, n_h=npc, n_a=n_a, nblk=nblk if h0 is not None else 0,
                          f_axis=len(grid) - 1),
        grid=grid,
        in_specs=in_specs,
        out_specs=out_spec,
        out_shape=out_shape,
        scratch_shapes=[pltpu.VMEM((TM, D), bf16)],
        compiler_params=_cparams(("parallel",) * (len(grid) - 1) + ("arbitrary",)),
        name="swiglu_ffn",
    )(*args)


TC = 512


def _conv_kernel(h_ref, hp_ref, hn_ref, g_ref, wb_ref, wc_ref, wu_ref, cw_ref, wo_ref, o_ref,
                 xn_ref, gu_ref):
    i = pl.program_id(0)

    @pl.when(pl.program_id(1) == 0)
    def _():
        x = h_ref[...]
        gain = g_ref[...]
        xn_ref[HALO:HALO + TM, :] = _rms(x, gain).astype(bf16)
        xn_ref[0:HALO, :] = _rms(hp_ref[...], gain).astype(bf16)
        nxt = _rms(hn_ref[...], gain)
        nxt = jnp.where(i == pl.num_programs(0) - 1, 0.0, nxt)
        xn_ref[HALO + TM:, :] = nxt.astype(bf16)
        o_ref[...] = x

    xn = xn_ref[...]
    gu_ref[...] = _dot(xn, wc_ref[...]) * _dot(xn, wu_ref[...])
    cw = cw_ref[...]
    conv = (gu_ref[HALO - 1:HALO - 1 + TM, :] * cw[0:1, :]
            + gu_ref[HALO:HALO + TM, :] * cw[1:2, :]
            + gu_ref[HALO + 1:HALO + 1 + TM, :] * cw[2:3, :])
    bgate = _dot(xn_ref[HALO:HALO + TM, :], wb_ref[...])
    o_ref[...] += _dot((bgate * conv).astype(bf16), wo_ref[...])


def _conv_mixer(h, g, w_in, conv_w, w_out):
    R = h.shape[0]
    nc = D // TC
    hb = TM // HALO
    last = R // HALO - 1
    return pl.pallas_call(
        _conv_kernel,
        grid=(R // TM, nc),
        in_specs=[
            pl.BlockSpec((TM, D), lambda i, c: (i, 0)),
            pl.BlockSpec((HALO, D), lambda i, c: (jnp.maximum(i * hb - 1, 0), 0)),
            pl.BlockSpec((HALO, D), lambda i, c: (jnp.minimum((i + 1) * hb, last), 0)),
            pl.BlockSpec((1, D), lambda i, c: (0, 0)),
            pl.BlockSpec((D, TC), lambda i, c: (0, c)),
            pl.BlockSpec((D, TC), lambda i, c: (0, nc + c)),
            pl.BlockSpec((D, TC), lambda i, c: (0, 2 * nc + c)),
            pl.BlockSpec((3, TC), lambda i, c: (0, c)),
            pl.BlockSpec((TC, D), lambda i, c: (c, 0)),
        ],
        out_specs=pl.BlockSpec((TM, D), lambda i, c: (i, 0)),
        out_shape=jax.ShapeDtypeStruct((R, D), f32),
        scratch_shapes=[pltpu.VMEM((TM + 2 * HALO, D), bf16), pltpu.VMEM((TM + 2 * HALO, TC), f32)],
        compiler_params=_cparams(("parallel", "arbitrary")),
        name="gated_conv",
    )(h, h, h, g.reshape(1, D), w_in, w_in, w_in, conv_w, w_out)


NJ = 16


def _dft_factors(L):
    best = None
    for l1 in range(16, min(L, 128) + 1, 16):
        if L % l1:
            continue
        l2p = -(-(L // l1) // NJ) * NJ
        cost = 4 * l1 * l1 * NJ * l2p + 8 * l2p * 256 * -(-l2p // 256) * l1
        if best is None or cost < best[0]:
            best = (cost, l1, L // l1, l2p)
    return best[1:]


def _dft_tables(L, L1, L2, L2p, shift):
    def cs(num, den):
        ang = (2.0 * math.pi / den) * (num % den).astype(f32)
        return jnp.cos(ang), jnp.sin(ang)

    i1 = jnp.arange(L1)
    c1, s1 = cs(i1[:, None] * i1[None, :], L1)
    eye = jnp.eye(NJ, dtype=f32)
    w1 = jnp.concatenate([jnp.kron(c1, eye), jnp.kron(-s1, eye)], axis=0).astype(bf16)
    n2 = jnp.arange(L2p)
    tc, ts = cs(i1[:, None] * n2[None, :], L)
    live = (n2 < L2)[None, :] * (L ** -0.5)

    def block_rows(t):
        return (t * live).reshape(L1, L2p // NJ, NJ).transpose(1, 0, 2).reshape(L2p // NJ, L1 * NJ, 1)

    tw = jnp.stack([block_rows(tc), block_rows(ts)], axis=1)
    k2 = n2 - shift
    c3, s3 = cs(k2[:, None] * n2[None, :], L2)
    inside = jnp.logical_and(jnp.logical_and(k2[:, None] >= 0, k2[:, None] < L2), n2[None, :] < L2)
    w3 = jnp.stack([jnp.where(inside, c3, 0.0), jnp.where(inside, s3, 0.0)]).astype(bf16)
    ic = jnp.arange(FGROUP)
    cc, sc = cs(ic[:, None] * ic[None, :], FGROUP)
    wc = (jnp.stack([cc, sc]) * (FGROUP ** -0.5)).astype(bf16)
    return w1, tw, w3, wc


def _dft1_kernel(x_ref, w_ref, tw_ref, tr_ref, ti_ref):
    l1, nj, dc = x_ref.shape
    rows = l1 * nj
    t = _dot(w_ref[...], x_ref[...].reshape(rows, dc))
    tr, ti = t[:rows], t[rows:]
    tc, ts = tw_ref[0], tw_ref[1]
    tr_ref[...] = (tr * tc + ti * ts).astype(bf16).reshape(l1, nj, dc)
    ti_ref[...] = (ti * tc - tr * ts).astype(bf16).reshape(l1, nj, dc)


def _dft3_kernel(tr_ref, ti_ref, w3_ref, wc_ref, o_ref):
    c3, s3 = w3_ref[0], w3_ref[1]
    tr, ti = tr_ref[...], ti_ref[...]
    yr = (_dot(c3, tr) + _dot(s3, ti)).astype(bf16)
    yi = (_dot(c3, ti) - _dot(s3, tr)).astype(bf16)
    cc, sc = wc_ref[0], wc_ref[1]
    for gidx in range(o_ref.shape[1] // FGROUP):
        sl = slice(gidx * FGROUP, (gidx + 1) * FGROUP)
        o_ref[:, sl] = (_dot(yr[:, sl], cc) + _dot(yi[:, sl], sc)).astype(bf16)


def _fourier_mix(hn, B, Lp):
    L = Lp - META0
    L1, L2, L2p = _dft_factors(L)
    direct = META0 % L1 == 0 and L1 * L2p == Lp
    shift = META0 // L1 if direct else 0
    w1, tw, w3, wc = _dft_tables(L, L1, L2, L2p, shift)
    x = hn.reshape(B, Lp, D)[:, META0:].reshape(B, L1, L2, D)
    x = jnp.pad(x, ((0, 0), (0, 0), (0, L2p - L2), (0, 0)))
    dc1 = 512
    rows = L1 * NJ
    t_shape = jax.ShapeDtypeStruct((B, L1, L2p, D), bf16)
    t_spec = pl.BlockSpec((None, L1, NJ, dc1), lambda b, n, c: (b, 0, n, c))
    tr, ti = pl.pallas_call(
        _dft1_kernel,
        grid=(B, L2p // NJ, D // dc1),
        in_specs=[
            t_spec,
            pl.BlockSpec((2 * rows, rows), lambda b, n, c: (0, 0)),
            pl.BlockSpec((None, 2, rows, 1), lambda b, n, c: (n, 0, 0, 0)),
        ],
        out_specs=[t_spec, t_spec],
        out_shape=[t_shape, t_shape],
        compiler_params=_cparams(("parallel", "parallel", "arbitrary")),
        name="seq_dft_stage1",
    )(x, w1, tw)
    dc3 = D if L2p <= 256 else 512
    nc3 = D // dc3
    slab = pl.BlockSpec((None, None, L2p, dc3), lambda b, k, c: (b, k, 0, c))
    mixed = pl.pallas_call(
        _dft3_kernel,
        grid=(B, L1, nc3),
        in_specs=[
            slab,
            slab,
            pl.BlockSpec((2, L2p, L2p), lambda b, k, c: (0, 0, 0)),
            pl.BlockSpec((2, FGROUP, FGROUP), lambda b, k, c: (0, 0, 0)),
        ],
        out_specs=pl.BlockSpec((None, L2p, dc3), lambda b, k, c: (b, 0, k * nc3 + c)),
        out_shape=jax.ShapeDtypeStruct((B, L2p, L1 * D), bf16),
        compiler_params=_cparams(("parallel", "parallel", "arbitrary")),
        name="seq_dft_stage2_channel_dft",
    )(tr, ti, w3, wc)
    if direct:
        return mixed.reshape(B * Lp, D)
    mixed = mixed.reshape(B, L2p * L1, D)[:, :L]
    return jnp.pad(mixed, ((0, 0), (META0, 0), (0, 0))).reshape(B * Lp, D)


def _encode(x, meta, p):
    B, S, _ = x.shape
    Lp = S + FRONT
    front = jnp.concatenate([jnp.zeros((META0, D), f32), meta.astype(f32)], axis=0)
    h0 = (B, S, front)
    x2d = x.reshape(B * S, D)

    qkv = _qkv(x2d, p["ln_mix"][0], *_rope_permute(p["a_wqkv"], p["a_q_norm"], p["a_k_norm"]),
               _rope_tables(S, B), h0=h0)
    h = _ffn(x2d, p["ln_ffn"][0], p["ffn_w_in"][0], p["ffn_w_out"][0], (_flash(qkv, B, Lp), p["a_wo"]),
             h0=h0)
    h = _ffn(h, p["ln_ffn"][1], p["ffn_w_in"][1], p["ffn_w_out"][1],
             (_fourier_mix(_norm(h, p["ln_mix"][1]), B, Lp), p["b_w"]))
    h = _conv_mixer(h, p["ln_mix"][2], p["c_w_in"], p["c_conv"], p["c_w_out"])
    h = _ffn(h, p["ln_ffn"][2], p["ffn_w_in"][2], p["ffn_w_out"][2])
    qkv = _qkv(h, p["ln_mix"][3], p["d_wqkv"], p["d_q_norm"], p["d_k_norm"], None)
    return _ffn(h, p["ln_ffn"][3], p["ffn_w_in"][3], p["ffn_w_out"][3],
                (_win_attention(qkv, p["d_sink"], B, Lp), p["d_wo"]), real=(B, S))


def kernel(x_prompt, x_sample, meta_tokens, ln_mix, ln_ffn, a_wqkv, a_q_norm, a_k_norm, a_wo, b_w,
           c_w_in, c_conv, c_w_out, d_wqkv, d_q_norm, d_k_norm, d_sink, d_wo, ffn_w_in, ffn_w_out):
    p = {
        "ln_mix": ln_mix, "ln_ffn": ln_ffn,
        "a_wqkv": a_wqkv[0].astype(bf16), "a_q_norm": a_q_norm[0], "a_k_norm": a_k_norm[0],
        "a_wo": a_wo[0].astype(bf16), "b_w": b_w[0].astype(bf16),
        "c_w_in": c_w_in[0].astype(bf16), "c_conv": c_conv[0], "c_w_out": c_w_out[0].astype(bf16),
        "d_wqkv": d_wqkv[0].astype(bf16), "d_q_norm": d_q_norm[0], "d_k_norm": d_k_norm[0],
        "d_sink": d_sink[0], "d_wo": d_wo[0].astype(bf16),
        "ffn_w_in": [ffn_w_in[i].astype(bf16) for i in range(ffn_w_in.shape[0])],
        "ffn_w_out": [ffn_w_out[i].astype(bf16) for i in range(ffn_w_out.shape[0])],
    }
    return _encode(x_prompt, meta_tokens, p), _encode(x_sample, meta_tokens, p)
```
